```python
import math
import jax, jax.numpy as jnp
from jax import lax
import numpy as np

D_MODEL = 1024
BATCH = 8
SEQ = 2048
DEPTH = 4

POOL_WIDTH = D_MODEL // 4
POOL_WINDOWS = (2, 4, 8, 16)
POOL_GROUPS = len(POOL_WINDOWS)
POOL_GROUP_DIM = POOL_WIDTH // POOL_GROUPS
CONV_WIDTH = D_MODEL // 4
CONV_K = 3
HEAD_DIM = 64
ATTN_HEADS = 8
ATTN_WIDTH = ATTN_HEADS * HEAD_DIM
ROPE_DIM = HEAD_DIM // 4
ROPE_THETA = 500000.0
MOBA_BLOCK = 256
MOBA_TOPK = 3
Q_CHUNK = 32
N_BRANCH = 3
IN_WIDTH = POOL_WIDTH + 3 * CONV_WIDTH + 3 * ATTN_WIDTH + N_BRANCH * D_MODEL
D_FF = -(-8 * D_MODEL // (3 * 256)) * 256
EPS = 1e-6
NEG = -1e30
POS_OFFSET_MAX = 1024

kernel_name = "hybrid_pool_conv_moba_adaln_block"


def rmsnorm(x, w):
    xf = x.astype(jnp.float32)
    xf = xf * lax.rsqrt(jnp.mean(xf * xf, axis=-1, keepdims=True) + EPS)
    return xf.astype(x.dtype) * w


def rotary_tables(positions):
    freqs = ROPE_THETA ** (-jnp.arange(0, ROPE_DIM, 2, dtype=jnp.float32) / ROPE_DIM)
    ang = positions.astype(jnp.float32)[..., None] * freqs
    return jnp.cos(ang), jnp.sin(ang)


def apply_partial_rope(x, cos, sin):
    half = ROPE_DIM // 2
    xr = x[..., :ROPE_DIM].astype(jnp.float32)
    x1, x2 = xr[..., :half], xr[..., half:]
    c = cos[:, :, None, :]
    s = sin[:, :, None, :]
    rot = jnp.concatenate([x1 * c - x2 * s, x2 * c + x1 * s], axis=-1).astype(x.dtype)
    return jnp.concatenate([rot, x[..., ROPE_DIM:]], axis=-1)


def pool_mixer(u, pool_w, pool_scale):
    B, S, _ = u.shape
    uf = u.astype(jnp.float32)
    cs = jnp.pad(jnp.cumsum(uf, axis=1), ((0, 0), (1, 0), (0, 0)))
    t = jnp.arange(S)
    outs = []
    for g, w in enumerate(POOL_WINDOWS):
        sl = slice(g * POOL_GROUP_DIM, (g + 1) * POOL_GROUP_DIM)
        lo = jnp.maximum(t + 1 - w, 0)
        cs_g = cs[..., sl]
        window_sum = cs_g[:, 1:] - jnp.take(cs_g, lo, axis=1)
        count = (t + 1 - lo).astype(jnp.float32)[None, :, None]
        outs.append(window_sum / count - uf[..., sl])
    pooled = jnp.concatenate(outs, axis=-1).astype(u.dtype).reshape(B, S, POOL_GROUPS, POOL_GROUP_DIM)
    y = jnp.einsum('bsgi,gio->bsgo', pooled, pool_w).reshape(B, S, POOL_WIDTH)
    return y * pool_scale


def causal_short_conv(u, conv_w):
    C = u.shape[-1]
    return lax.conv_general_dilated(
        u, conv_w[:, None, :], window_strides=(1,), padding=((CONV_K - 1, 0),),
        dimension_numbers=('NWC', 'WIO', 'NWC'), feature_group_count=C)


def moba_attention(q, k, v):
    B, H, S, hd = q.shape
    nb = -(-S // MOBA_BLOCK)
    pad = nb * MOBA_BLOCK - S
    kb = jnp.pad(k, ((0, 0), (0, 0), (0, pad), (0, 0))).reshape(B, H, nb, MOBA_BLOCK, hd)
    vb = jnp.pad(v, ((0, 0), (0, 0), (0, pad), (0, 0))).reshape(B, H, nb, MOBA_BLOCK, hd)
    kmean = jnp.mean(kb.astype(jnp.float32), axis=3)
    n_sel = max(1, min(MOBA_TOPK, nb - 1))
    nc = S // Q_CHUNK
    q_chunks = q.reshape(B, H, nc, Q_CHUNK, hd).transpose(2, 0, 1, 3, 4)
    starts = jnp.arange(nc, dtype=jnp.int32) * Q_CHUNK
    scale = 1.0 / math.sqrt(hd)
    gather_blocks = jax.vmap(jax.vmap(lambda blocks, ix: blocks[ix]))

    def one_chunk(args):
        q_i, start = args
        qpos = start + jnp.arange(Q_CHUNK, dtype=jnp.int32)
        blk = start // MOBA_BLOCK
        gate = jnp.einsum('bhqd,bhnd->bhqn', q_i.astype(jnp.float32), kmean)
        past = jnp.arange(nb) < blk
        gate = jnp.where(past, gate, NEG)
        _, idx = lax.top_k(gate, n_sel)
        valid = idx < blk
        k_sel = gather_blocks(kb, idx)
        v_sel = gather_blocks(vb, idx)
        s_sel = jnp.einsum('bhqd,bhqjkd->bhqjk', q_i, k_sel).astype(jnp.float32) * scale
        s_sel = jnp.where(valid[..., None], s_sel, NEG).reshape(B, H, Q_CHUNK, n_sel * MOBA_BLOCK)
        k_own = lax.dynamic_index_in_dim(kb, blk, axis=2, keepdims=False)
        v_own = lax.dynamic_index_in_dim(vb, blk, axis=2, keepdims=False)
        s_own = jnp.einsum('bhqd,bhkd->bhqk', q_i, k_own).astype(jnp.float32) * scale
        kpos = blk * MOBA_BLOCK + jnp.arange(MOBA_BLOCK, dtype=jnp.int32)
        s_own = jnp.where(kpos[None, :] <= qpos[:, None], s_own, NEG)
        p = jax.nn.softmax(jnp.concatenate([s_sel, s_own], axis=-1), axis=-1).astype(v.dtype)
        p_sel = p[..., :n_sel * MOBA_BLOCK].reshape(B, H, Q_CHUNK, n_sel, MOBA_BLOCK)
        p_own = p[..., n_sel * MOBA_BLOCK:]
        return (jnp.einsum('bhqjk,bhqjkd->bhqd', p_sel, v_sel)
                + jnp.einsum('bhqk,bhkd->bhqd', p_own, v_own))

    out = lax.map(one_chunk, (q_chunks, starts))
    return out.transpose(1, 2, 0, 3, 4).reshape(B, H, S, hd)


def hybrid_layer(x, c_act, cos, sin, norm1_w, norm2_w, w_ada, b_ada, w_in, pool_w, pool_scale,
                 conv_w, q_norm_w, k_norm_w, p_pool, p_conv, p_attn, w_out, w_gate, w_up, w_down):
    B, S, D = x.shape
    mod = (c_act @ w_ada + b_ada)[:, None, :]
    sh1, sc1, g1, sh2, sc2, g2 = jnp.split(mod, 6, axis=-1)
    h = rmsnorm(x, norm1_w) * (1 + sc1) + sh1
    u = h @ w_in
    widths = [POOL_WIDTH, CONV_WIDTH, CONV_WIDTH, CONV_WIDTH, ATTN_WIDTH, ATTN_WIDTH, ATTN_WIDTH]
    offsets = [int(o) for o in np.cumsum(widths)]
    u_pool, u_conv, b_conv, c_conv, q, k, v, gate_logits = jnp.split(u, offsets, axis=-1)
    y_pool = pool_mixer(u_pool, pool_w, pool_scale)
    y_conv = b_conv * causal_short_conv(c_conv * u_conv, conv_w)
    q = apply_partial_rope(rmsnorm(q.reshape(B, S, ATTN_HEADS, HEAD_DIM), q_norm_w), cos, sin)
    k = apply_partial_rope(rmsnorm(k.reshape(B, S, ATTN_HEADS, HEAD_DIM), k_norm_w), cos, sin)
    v = v.reshape(B, S, ATTN_HEADS, HEAD_DIM)
    o = moba_attention(q.transpose(0, 2, 1, 3), k.transpose(0, 2, 1, 3), v.transpose(0, 2, 1, 3))
    y_attn = o.transpose(0, 2, 1, 3).reshape(B, S, ATTN_WIDTH)
    gates = jax.nn.sigmoid(gate_logits).reshape(B, S, N_BRANCH, D)
    merged = (gates[:, :, 0] * (y_pool @ p_pool)
              + gates[:, :, 1] * (y_conv @ p_conv)
              + gates[:, :, 2] * (y_attn @ p_attn))
    x = x + g1 * (merged @ w_out)
    h2 = rmsnorm(x, norm2_w) * (1 + sc2) + sh2
    x = x + g2 * ((jax.nn.silu(h2 @ w_gate) * (h2 @ w_up)) @ w_down)
    return x


def setup_inputs(seed: int = 0) -> dict:
    key = jax.random.key(seed)
    ks = jax.random.split(key, 24)
    L = DEPTH

    def nrm(k, shape, fan_in, gain=1.0):
        return gain * fan_in ** -0.5 * jax.random.normal(k, shape, jnp.float32)

    def gain_init(k, shape):
        return 1.0 + 0.1 * jax.random.normal(k, shape, jnp.float32)

    x = jax.random.normal(ks[0], (BATCH, SEQ, D_MODEL), jnp.float32)
    c = jax.random.normal(ks[1], (BATCH, D_MODEL), jnp.float32)
    positions = (jnp.arange(SEQ, dtype=jnp.int32)[None, :]
                 + jax.random.randint(ks[2], (BATCH, 1), 0, POS_OFFSET_MAX, dtype=jnp.int32))
    return {
        "x": x,
        "c": c,
        "positions": positions,
        "norm1_w": gain_init(ks[3], (L, D_MODEL)),
        "norm2_w": gain_init(ks[4], (L, D_MODEL)),
        "w_ada": nrm(ks[5], (L, D_MODEL, 6 * D_MODEL), D_MODEL, 0.5),
        "b_ada": 0.02 * jax.random.normal(ks[6], (L, 6 * D_MODEL), jnp.float32),
        "w_in": nrm(ks[7], (L, D_MODEL, IN_WIDTH), D_MODEL),
        "pool_w": nrm(ks[8], (L, POOL_GROUPS, POOL_GROUP_DIM, POOL_GROUP_DIM), POOL_GROUP_DIM),
        "pool_scale": gain_init(ks[9], (L, POOL_WIDTH)),
        "conv_w": nrm(ks[10], (L, CONV_K, CONV_WIDTH), CONV_K),
        "q_norm_w": gain_init(ks[11], (L, HEAD_DIM)),
        "k_norm_w": gain_init(ks[12], (L, HEAD_DIM)),
        "p_pool": nrm(ks[13], (L, POOL_WIDTH, D_MODEL), POOL_WIDTH),
        "p_conv": nrm(ks[14], (L, CONV_WIDTH, D_MODEL), CONV_WIDTH),
        "p_attn": nrm(ks[15], (L, ATTN_WIDTH, D_MODEL), ATTN_WIDTH),
        "w_out": nrm(ks[16], (L, D_MODEL, D_MODEL), D_MODEL),
        "w_gate": nrm(ks[17], (L, D_MODEL, D_FF), D_MODEL),
        "w_up": nrm(ks[18], (L, D_MODEL, D_FF), D_MODEL),
        "w_down": nrm(ks[19], (L, D_FF, D_MODEL), D_FF),
    }


def reference(x, c, positions, norm1_w, norm2_w, w_ada, b_ada, w_in, pool_w, pool_scale, conv_w,
              q_norm_w, k_norm_w, p_pool, p_conv, p_attn, w_out, w_gate, w_up, w_down):
    c_act = jax.nn.silu(c)
    cos, sin = rotary_tables(positions)
    for l in range(DEPTH):
        x = hybrid_layer(x, c_act, cos, sin, norm1_w[l], norm2_w[l], w_ada[l], b_ada[l], w_in[l],
                         pool_w[l], pool_scale[l], conv_w[l], q_norm_w[l], k_norm_w[l],
                         p_pool[l], p_conv[l], p_attn[l], w_out[l], w_gate[l], w_up[l], w_down[l])
    return x
```

```python
import functools
import math

import numpy as np
import jax
import jax.numpy as jnp
from jax import lax
from jax.experimental import pallas as pl
from jax.experimental.pallas import tpu as pltpu

F32 = jnp.float32
BF16 = jnp.bfloat16

HEAD_DIM = 64
ATTN_HEADS = 8
ATTN_WIDTH = ATTN_HEADS * HEAD_DIM
ROPE_DIM = HEAD_DIM // 4
ROPE_HALF = ROPE_DIM // 2
ROPE_THETA = 500000.0
MOBA_BLOCK = 256
MOBA_TOPK = 3
POOL_WINDOWS = (2, 4, 8, 16)
CONV_K = 3
EPS = 1e-6
NEG = -1e30

LANES = 128
PAIR = LANES
HALO = 16
VMEM_LIMIT = 56 * 1024 * 1024

TM = 512
ADA_TN = 1536
Q_SCALE = (1.0 / math.sqrt(HEAD_DIM)) * math.log2(math.e)


def _dot(a, b):
    return jnp.dot(a, b, preferred_element_type=F32)


def _const_spec(shape):
    nd = len(shape)
    return pl.BlockSpec(shape, lambda *_: (0,) * nd, pipeline_mode=pl.Buffered(1))


def _params(n_axes):
    return pltpu.CompilerParams(dimension_semantics=("arbitrary",) * n_axes,
                                vmem_limit_bytes=VMEM_LIMIT)


def _mod_kernel(c_ref, w_ref, b_ref, o_ref):
    c = c_ref[...]
    c_act = (c * jax.nn.sigmoid(c)).astype(BF16)
    o_ref[0] = _dot(c_act, w_ref[0].astype(BF16)) + b_ref[0]


def _modulation(c, w_ada, b_ada):
    L, D, N = w_ada.shape
    B = c.shape[0]
    return pl.pallas_call(
        _mod_kernel,
        grid=(L, N // ADA_TN),
        in_specs=[pl.BlockSpec((B, D), lambda l, j: (0, 0)),
                  pl.BlockSpec((1, D, ADA_TN), lambda l, j: (l, 0, j)),
                  pl.BlockSpec((1, 1, ADA_TN), lambda l, j: (l, 0, j))],
        out_specs=pl.BlockSpec((1, B, ADA_TN), lambda l, j: (l, 0, j)),
        out_shape=jax.ShapeDtypeStruct((L, B, N), F32),
        compiler_params=_params(2),
        name="adaln_modulation",
    )(c, w_ada, b_ada.reshape(L, 1, N))


def _rope_kernel(pos_ref, freq_ref, c_ref, sa_ref, sb_ref):
    ang = pos_ref[0].astype(F32) * freq_ref[...]
    cs = jnp.cos(ang)
    sn = jnp.sin(ang)
    d = lax.broadcasted_iota(jnp.int32, (1, LANES), 1) & (HEAD_DIM - 1)
    c_ref[0] = jnp.where(d < ROPE_DIM, cs, 1.0)
    sa_ref[0] = jnp.where(d < ROPE_HALF, -sn, 0.0)
    sb_ref[0] = jnp.where((d >= ROPE_HALF) & (d < ROPE_DIM), sn, 0.0)


def _rope_tables(positions):
    B, S = positions.shape
    d = np.arange(LANES) % HEAD_DIM
    freq = np.where(d < ROPE_DIM, ROPE_THETA ** (-(2.0 * (d % ROPE_HALF)) / ROPE_DIM), 0.0)
    freq = jnp.asarray(freq.reshape(1, LANES), F32)
    spec = pl.BlockSpec((1, S, LANES), lambda b: (b, 0, 0))
    shp = jax.ShapeDtypeStruct((B, S, LANES), F32)
    return pl.pallas_call(
        _rope_kernel,
        grid=(B,),
        in_specs=[pl.BlockSpec((1, S, 1), lambda b: (b, 0, 0)),
                  pl.BlockSpec((1, LANES), lambda b: (0, 0))],
        out_specs=(spec, spec, spec),
        out_shape=(shp, shp, shp),
        compiler_params=_params(1),
        name="rotary_tables",
    )(positions.reshape(B, S, 1), freq)


def _in_kernel(x_ref, sh_ref, sc_ref, n1w_ref, win_ref, qw_ref, kw_ref, ind_ref,
               rc_ref, rsa_ref, rsb_ref, pwbd_ref, pscale_ref, convw_ref, ppool_ref, pconv_ref,
               part_ref, ga_ref, qT_ref, k_ref, vT_ref, km_ref, ext_ref,
               *, d_model, pool_w, conv_w, attn_w):
    si = pl.program_id(1)
    x = x_ref[0]
    ms = jnp.mean(x * x, axis=-1, keepdims=True)
    h = (x * lax.rsqrt(ms + EPS) * n1w_ref[...]) * (1.0 + sc_ref[0]) + sh_ref[0]
    h = h.astype(BF16)

    @pl.when(si == 0)
    def _():
        ext_ref[0:HALO, :] = jnp.zeros((HALO, pool_w + conv_w), F32)

    o_pool = 0
    o_uc = o_pool + pool_w
    o_bc = o_uc + conv_w
    o_cc = o_bc + conv_w
    o_q = o_cc + conv_w
    o_k = o_q + attn_w
    o_v = o_k + attn_w
    o_g = o_v + attn_w

    ua = _dot(h, win_ref[:, o_pool:o_q])
    up = ua[:, o_pool:o_uc]
    uc = ua[:, o_uc:o_bc]
    bc = ua[:, o_bc:o_cc]
    cc = ua[:, o_cc:o_q]
    ext_ref[HALO:HALO + TM, 0:pool_w] = up
    ext_ref[HALO:HALO + TM, pool_w:pool_w + conv_w] = cc * uc
    ext = ext_ref[...]
    ext_ref[0:HALO, :] = ext[TM:TM + HALO, :]
    e_p = ext[:, 0:pool_w]
    e_c = ext[:, pool_w:pool_w + conv_w]

    s2 = e_p + pltpu.roll(e_p, 1, 0)
    s4 = s2 + pltpu.roll(s2, 2, 0)
    s8 = s4 + pltpu.roll(s4, 4, 0)
    s16 = s8 + pltpu.roll(s8, 8, 0)
    gdim = pool_w // len(POOL_WINDOWS)
    lane = lax.broadcasted_iota(jnp.int32, (1, pool_w), 1)
    wsum = jnp.where(lane < gdim, s2, jnp.where(lane < 2 * gdim, s4, jnp.where(lane < 3 * gdim, s8, s16)))
    win = jnp.where(lane < gdim, POOL_WINDOWS[0],
                    jnp.where(lane < 2 * gdim, POOL_WINDOWS[1],
                              jnp.where(lane < 3 * gdim, POOL_WINDOWS[2], POOL_WINDOWS[3])))
    t = si * TM + lax.broadcasted_iota(jnp.int32, (TM, pool_w), 0)
    cnt = jnp.minimum(t + 1, win).astype(F32)
    pooled = wsum[HALO:, :] / cnt - up
    yp = _dot(pooled.astype(BF16), pwbd_ref[...]) * pscale_ref[...]
    yp = _dot(yp.astype(BF16), ppool_ref[...])

    cw = convw_ref[...]
    conv = cw[0:1, :] * pltpu.roll(e_c, 2, 0) + cw[1:2, :] * pltpu.roll(e_c, 1, 0) + cw[2:3, :] * e_c
    yc = bc * conv[HALO:, :]
    yc = _dot(yc.astype(BF16), pconv_ref[...])

    g0 = jax.nn.sigmoid(_dot(h, win_ref[:, o_g:o_g + d_model]))
    g1 = jax.nn.sigmoid(_dot(h, win_ref[:, o_g + d_model:o_g + 2 * d_model]))
    part_ref[0] = (g0 * yp + g1 * yc).astype(BF16)
    ga_ref[0] = jax.nn.sigmoid(_dot(h, win_ref[:, o_g + 2 * d_model:o_g + 3 * d_model])).astype(BF16)

    rc = rc_ref[0]
    rsa = rsa_ref[0]
    rsb = rsb_ref[0]

    def norm_rope(u, w_tiled):
        ss = _dot((u * u).astype(BF16), ind_ref[...])
        un = u * lax.rsqrt(ss + EPS) * w_tiled
        outs = []
        for p in range(attn_w // PAIR):
            cp = un[:, p * PAIR:(p + 1) * PAIR]
            outs.append(cp * rc + pltpu.roll(cp, PAIR - ROPE_HALF, 1) * rsa + pltpu.roll(cp, ROPE_HALF, 1) * rsb)
        return outs

    uq = _dot(h, win_ref[:, o_q:o_k])
    for p, qp in enumerate(norm_rope(uq, qw_ref[...])):
        qT_ref[0, p * PAIR:(p + 1) * PAIR, :] = (qp * Q_SCALE).T.astype(BF16)

    uk = _dot(h, win_ref[:, o_k:o_v])
    for p, kp in enumerate(norm_rope(uk, kw_ref[...])):
        k_ref[0, :, p * PAIR:(p + 1) * PAIR] = kp.astype(BF16)
        for j in range(TM // MOBA_BLOCK):
            km_ref[0, 0, j:j + 1, p * PAIR:(p + 1) * PAIR] = jnp.mean(
                kp[j * MOBA_BLOCK:(j + 1) * MOBA_BLOCK, :], axis=0, keepdims=True)

    uv = _dot(h, win_ref[:, o_v:o_g])
    for p in range(attn_w // PAIR):
        vT_ref[0, p * PAIR:(p + 1) * PAIR, :] = uv[:, p * PAIR:(p + 1) * PAIR].T.astype(BF16)


def _in_call(x, sh1, sc1, n1w, w_in, qw, kw, ind, rc, rsa, rsb, pwbd, pscale, convw, ppool, pconv):
    B, S, D = x.shape
    pool_w = pwbd.shape[0]
    conv_w = convw.shape[1]
    A = ATTN_WIDTH
    nt = S // TM
    row = lambda b, s: (b, s, 0)
    vec = lambda b, s: (b, 0, 0)
    tr = lambda b, s: (b, 0, s)
    kern = functools.partial(_in_kernel, d_model=D, pool_w=pool_w, conv_w=conv_w, attn_w=A)
    return pl.pallas_call(
        kern,
        grid=(B, nt),
        in_specs=[pl.BlockSpec((1, TM, D), row),
                  pl.BlockSpec((1, 1, D), vec), pl.BlockSpec((1, 1, D), vec),
                  _const_spec((1, D)), _const_spec(w_in.shape),
                  _const_spec((1, A)), _const_spec((1, A)), _const_spec((A, A)),
                  pl.BlockSpec((1, TM, LANES), row), pl.BlockSpec((1, TM, LANES), row),
                  pl.BlockSpec((1, TM, LANES), row),
                  _const_spec(pwbd.shape), _const_spec((1, pool_w)), _const_spec(convw.shape),
                  _const_spec(ppool.shape), _const_spec(pconv.shape)],
        out_specs=(pl.BlockSpec((1, TM, D), row), pl.BlockSpec((1, TM, D), row),
                   pl.BlockSpec((1, A, TM), tr), pl.BlockSpec((1, TM, A), row),
                   pl.BlockSpec((1, A, TM), tr),
                   pl.BlockSpec((1, 1, TM // MOBA_BLOCK, A), lambda b, s: (b, s, 0, 0))),
        out_shape=(jax.ShapeDtypeStruct((B, S, D), BF16), jax.ShapeDtypeStruct((B, S, D), BF16),
                   jax.ShapeDtypeStruct((B, A, S), BF16), jax.ShapeDtypeStruct((B, S, A), BF16),
                   jax.ShapeDtypeStruct((B, A, S), BF16),
                   jax.ShapeDtypeStruct((B, nt, TM // MOBA_BLOCK, A), F32)),
        scratch_shapes=[pltpu.VMEM((TM + HALO, pool_w + conv_w), F32)],
        compiler_params=_params(2),
        name="in_proj_mixers",
    )(x, sh1, sc1, n1w, w_in, qw, kw, ind, rc, rsa, rsb, pwbd, pscale, convw, ppool, pconv)


def _attn_kernel(qT_ref, k_ref, vT_ref, km_ref, o_ref, bias_ref, oT_ref, *, n_blocks):
    qi = pl.program_id(1)
    TQ = MOBA_BLOCK
    km = km_ref[0]
    blk = lax.broadcasted_iota(jnp.int32, (n_blocks, TQ), 0)
    past = blk < qi
    kpos = lax.broadcasted_iota(jnp.int32, (TQ, TQ), 0)
    qpos = lax.broadcasted_iota(jnp.int32, (TQ, TQ), 1)
    causal = kpos <= qpos
    half = lax.broadcasted_iota(jnp.int32, (PAIR, TQ), 0) // HEAD_DIM
    own = pl.multiple_of(qi * TQ, TQ)

    for h in range(ATTN_HEADS):
        p, hh = divmod(h, PAIR // HEAD_DIM)
        lanes = slice(p * PAIR, (p + 1) * PAIR)
        rows = slice(h * HEAD_DIM, (h + 1) * HEAD_DIM)
        qT_pair = qT_ref[0, lanes, :]
        qT_h = jnp.where(half == hh, qT_pair, jnp.zeros_like(qT_pair))

        km_pair = km[:, lanes]
        km_hi = km_pair.astype(BF16)
        km_lo = (km_pair - km_hi.astype(F32)).astype(BF16)
        gate = _dot(km_hi, qT_h) + _dot(km_lo, qT_h)
        gate = jnp.where(past, gate, NEG)
        beaten = jnp.zeros((n_blocks, TQ), jnp.int32)
        for m in range(n_blocks):
            gm = gate[m:m + 1, :]
            ahead = (gm > gate) | ((gm == gate) & (m < blk))
            beaten = beaten + ahead.astype(jnp.int32)
        sel = past & (beaten < MOBA_TOPK)
        bias_ref[...] = jnp.where(sel, 0.0, NEG)

        s = _dot(k_ref[0, pl.ds(own, TQ), lanes], qT_h)
        s = jnp.where(causal, s, NEG)
        m0 = jnp.max(s, axis=0, keepdims=True)
        pr = jnp.exp2(s - m0)
        l0 = jnp.sum(pr, axis=0, keepdims=True)
        acc0 = _dot(vT_ref[0, rows, pl.ds(own, TQ)], pr.astype(BF16))

        def body(n, carry):
            m_i, l_i, acc = carry
            off = pl.multiple_of(n * TQ, TQ)
            s = _dot(k_ref[0, pl.ds(off, TQ), lanes], qT_h) + bias_ref[pl.ds(n, 1), :]
            m_n = jnp.maximum(m_i, jnp.max(s, axis=0, keepdims=True))
            alpha = jnp.exp2(m_i - m_n)
            pr = jnp.exp2(s - m_n)
            l_n = alpha * l_i + jnp.sum(pr, axis=0, keepdims=True)
            acc_n = alpha * acc + _dot(vT_ref[0, rows, pl.ds(off, TQ)], pr.astype(BF16))
            return m_n, l_n, acc_n

        _, l_f, acc_f = lax.fori_loop(0, qi, body, (m0, l0, acc0))
        oT_ref[rows, :] = acc_f / l_f

    o_ref[0] = oT_ref[...].T.astype(BF16)


def _attn_call(qT, k, vT, km):
    B, A, S = qT.shape
    nb = S // MOBA_BLOCK
    TQ = MOBA_BLOCK
    kern = functools.partial(_attn_kernel, n_blocks=nb)
    return pl.pallas_call(
        kern,
        grid=(B, nb),
        in_specs=[pl.BlockSpec((1, A, TQ), lambda b, i: (b, 0, i)),
                  pl.BlockSpec((1, S, A), lambda b, i: (b, 0, 0)),
                  pl.BlockSpec((1, A, S), lambda b, i: (b, 0, 0)),
                  pl.BlockSpec((1, nb, A), lambda b, i: (b, 0, 0))],
        out_specs=pl.BlockSpec((1, TQ, A), lambda b, i: (b, i, 0)),
        out_shape=jax.ShapeDtypeStruct((B, S, A), BF16),
        scratch_shapes=[pltpu.VMEM((nb, TQ), F32), pltpu.VMEM((A, TQ), F32)],
        compiler_params=_params(2),
        name="moba_attention",
    )(qT, k, vT, km)


def _post_kernel(x_ref, part_ref, ga_ref, ya_ref, g1_ref, sh_ref, sc_ref, g2_ref, n2w_ref,
                 pattn_ref, wout_ref, wg_ref, wu_ref, wd_ref, o_ref, act_ref, *, ff_chunk):
    merged = part_ref[0].astype(F32) + ga_ref[0].astype(F32) * _dot(ya_ref[0], pattn_ref[...])
    x1 = x_ref[0] + g1_ref[0] * _dot(merged.astype(BF16), wout_ref[...])
    ms = jnp.mean(x1 * x1, axis=-1, keepdims=True)
    h2 = (x1 * lax.rsqrt(ms + EPS) * n2w_ref[...]) * (1.0 + sc_ref[0]) + sh_ref[0]
    h2 = h2.astype(BF16)
    d_ff = wg_ref.shape[1]
    for j in range(d_ff // ff_chunk):
        cols = slice(j * ff_chunk, (j + 1) * ff_chunk)
        g = _dot(h2, wg_ref[:, cols])
        u = _dot(h2, wu_ref[:, cols])
        act_ref[:, cols] = (g * jax.nn.sigmoid(g) * u).astype(BF16)
    o_ref[0] = x1 + g2_ref[0] * _dot(act_ref[...], wd_ref[...])


def _post_call(x, part, ga, ya, g1, sh2, sc2, g2, n2w, pattn, wout, wg, wu, wd):
    B, S, D = x.shape
    A = ya.shape[-1]
    d_ff = wg.shape[1]
    ff_chunk = 256
    assert d_ff % ff_chunk == 0
    row = lambda b, s: (b, s, 0)
    vec = lambda b, s: (b, 0, 0)
    kern = functools.partial(_post_kernel, ff_chunk=ff_chunk)
    return pl.pallas_call(
        kern,
        grid=(B, S // TM),
        in_specs=[pl.BlockSpec((1, TM, D), row), pl.BlockSpec((1, TM, D), row),
                  pl.BlockSpec((1, TM, D), row), pl.BlockSpec((1, TM, A), row),
                  pl.BlockSpec((1, 1, D), vec), pl.BlockSpec((1, 1, D), vec),
                  pl.BlockSpec((1, 1, D), vec), pl.BlockSpec((1, 1, D), vec),
                  _const_spec((1, D)),
                  _const_spec(pattn.shape), _const_spec(wout.shape), _const_spec(wg.shape),
                  _const_spec(wu.shape), _const_spec(wd.shape)],
        out_specs=pl.BlockSpec((1, TM, D), row),
        out_shape=jax.ShapeDtypeStruct((B, S, D), F32),
        scratch_shapes=[pltpu.VMEM((TM, d_ff), BF16)],
        compiler_params=_params(2),
        name="merge_swiglu",
    )(x, part, ga, ya, g1, sh2, sc2, g2, n2w, pattn, wout, wg, wu, wd)


def kernel(x, c, positions, norm1_w, norm2_w, w_ada, b_ada, w_in, pool_w, pool_scale, conv_w,
           q_norm_w, k_norm_w, p_pool, p_conv, p_attn, w_out, w_gate, w_up, w_down):
    B, S, D = x.shape
    L = w_ada.shape[0]
    assert S % TM == 0 and TM % MOBA_BLOCK == 0 and HALO >= max(POOL_WINDOWS) - 1
    n_groups, gdim = pool_w.shape[1], pool_w.shape[2]
    assert n_groups == len(POOL_WINDOWS) and conv_w.shape[1] == CONV_K

    mod = _modulation(c, w_ada, b_ada).reshape(L, B, 6, 1, D)
    rc, rsa, rsb = _rope_tables(positions)

    head = np.arange(ATTN_WIDTH) // HEAD_DIM
    ind = jnp.asarray((head[:, None] == head[None, :]) / HEAD_DIM, BF16)
    heads_per_w = ATTN_WIDTH // HEAD_DIM

    for l in range(L):
        pwbd = jnp.zeros((n_groups * gdim, n_groups * gdim), F32)
        for g in range(n_groups):
            pwbd = pwbd.at[g * gdim:(g + 1) * gdim, g * gdim:(g + 1) * gdim].set(pool_w[l, g])
        part, ga, qT, k, vT, km = _in_call(
            x, mod[l, :, 0], mod[l, :, 1], norm1_w[l].reshape(1, D), w_in[l].astype(BF16),
            jnp.tile(q_norm_w[l], heads_per_w).reshape(1, ATTN_WIDTH),
            jnp.tile(k_norm_w[l], heads_per_w).reshape(1, ATTN_WIDTH), ind,
            rc, rsa, rsb, pwbd.astype(BF16), pool_scale[l].reshape(1, -1), conv_w[l],
            p_pool[l].astype(BF16), p_conv[l].astype(BF16))
        ya = _attn_call(qT, k, vT, km.reshape(B, S // MOBA_BLOCK, ATTN_WIDTH))
        x = _post_call(x, part, ga, ya, mod[l, :, 2], mod[l, :, 3], mod[l, :, 4], mod[l, :, 5],
                       norm2_w[l].reshape(1, D), p_attn[l].astype(BF16), w_out[l].astype(BF16),
                       w_gate[l].astype(BF16), w_up[l].astype(BF16), w_down[l].astype(BF16))
    return x
```

```python
import functools
import math

import numpy as np
import jax
import jax.numpy as jnp
from jax import lax
from jax.experimental import pallas as pl
from jax.experimental.pallas import tpu as pltpu

F32 = jnp.float32
BF16 = jnp.bfloat16

HEAD_DIM = 64
ATTN_HEADS = 8
ATTN_WIDTH = ATTN_HEADS * HEAD_DIM
ROPE_DIM = HEAD_DIM // 4
ROPE_HALF = ROPE_DIM // 2
ROPE_THETA = 500000.0
MOBA_BLOCK = 256
MOBA_TOPK = 3
POOL_WINDOWS = (2, 4, 8, 16)
CONV_K = 3
EPS = 1e-6
NEG = -1e30

LANES = 128
PAIR = LANES
HALO = 16
VMEM_LIMIT = 56 * 1024 * 1024

TM = 512
ADA_TN = 1536
QK_AHEAD = 5
Q_SCALE = (1.0 / math.sqrt(HEAD_DIM)) * math.log2(math.e)


def _dot(a, b):
    return jnp.dot(a, b, preferred_element_type=F32)


def _const_spec(shape):
    nd = len(shape)
    return pl.BlockSpec(shape, lambda *_: (0,) * nd, pipeline_mode=pl.Buffered(1))


def _params(n_axes):
    return pltpu.CompilerParams(dimension_semantics=("arbitrary",) * n_axes,
                                vmem_limit_bytes=VMEM_LIMIT)


def _mod_kernel(c_ref, w_ref, b_ref, o_ref):
    c = c_ref[...]
    c_act = (c * jax.nn.sigmoid(c)).astype(BF16)
    o_ref[0] = _dot(c_act, w_ref[0].astype(BF16)) + b_ref[0]


def _modulation(c, w_ada, b_ada):
    L, D, N = w_ada.shape
    B = c.shape[0]
    return pl.pallas_call(
        _mod_kernel,
        grid=(L, N // ADA_TN),
        in_specs=[pl.BlockSpec((B, D), lambda l, j: (0, 0)),
                  pl.BlockSpec((1, D, ADA_TN), lambda l, j: (l, 0, j)),
                  pl.BlockSpec((1, 1, ADA_TN), lambda l, j: (l, 0, j))],
        out_specs=pl.BlockSpec((1, B, ADA_TN), lambda l, j: (l, 0, j)),
        out_shape=jax.ShapeDtypeStruct((L, B, N), F32),
        compiler_params=_params(2),
        name="adaln_modulation",
    )(c, w_ada, b_ada.reshape(L, 1, N))


def _rope_kernel(pos_ref, freq_ref, c_ref, sa_ref, sb_ref):
    ang = pos_ref[0].astype(F32) * freq_ref[...]
    cs = jnp.cos(ang)
    sn = jnp.sin(ang)
    d = lax.broadcasted_iota(jnp.int32, (1, LANES), 1) & (HEAD_DIM - 1)
    c_ref[0] = jnp.where(d < ROPE_DIM, cs, 1.0)
    sa_ref[0] = jnp.where(d < ROPE_HALF, -sn, 0.0)
    sb_ref[0] = jnp.where((d >= ROPE_HALF) & (d < ROPE_DIM), sn, 0.0)


def _rope_tables(positions):
    B, S = positions.shape
    d = np.arange(LANES) % HEAD_DIM
    freq = np.where(d < ROPE_DIM, ROPE_THETA ** (-(2.0 * (d % ROPE_HALF)) / ROPE_DIM), 0.0)
    freq = jnp.asarray(freq.reshape(1, LANES), F32)
    spec = pl.BlockSpec((1, S, LANES), lambda b: (b, 0, 0))
    shp = jax.ShapeDtypeStruct((B, S, LANES), F32)
    return pl.pallas_call(
        _rope_kernel,
        grid=(B,),
        in_specs=[pl.BlockSpec((1, S, 1), lambda b: (b, 0, 0)),
                  pl.BlockSpec((1, LANES), lambda b: (0, 0))],
        out_specs=(spec, spec, spec),
        out_shape=(shp, shp, shp),
        compiler_params=_params(1),
        name="rotary_tables",
    )(positions.reshape(B, S, 1), freq)


def _in_kernel(x_ref, sh_ref, sc_ref, n1w_ref, win_ref, qw_ref, kw_ref, ind_ref,
               rc_ref, rsa_ref, rsb_ref, pwbd_ref, pscale_ref, convw_ref, ppool_ref, pconv_ref,
               part_ref, ga_ref, qT_ref, k_ref, vT_ref, km_ref, ext_ref,
               *, d_model, pool_w, conv_w, attn_w):
    si = pl.program_id(1)
    x = x_ref[0]
    ms = jnp.mean(x * x, axis=-1, keepdims=True)
    h = (x * lax.rsqrt(ms + EPS) * n1w_ref[...]) * (1.0 + sc_ref[0]) + sh_ref[0]
    h = h.astype(BF16)

    @pl.when(si == 0)
    def _():
        ext_ref[0:HALO, :] = jnp.zeros((HALO, pool_w + conv_w), F32)

    o_pool = 0
    o_uc = o_pool + pool_w
    o_bc = o_uc + conv_w
    o_cc = o_bc + conv_w
    o_q = o_cc + conv_w
    o_k = o_q + attn_w
    o_v = o_k + attn_w
    o_g = o_v + attn_w

    ua = _dot(h, win_ref[:, o_pool:o_q])
    up = ua[:, o_pool:o_uc]
    uc = ua[:, o_uc:o_bc]
    bc = ua[:, o_bc:o_cc]
    cc = ua[:, o_cc:o_q]
    ext_ref[HALO:HALO + TM, 0:pool_w] = up
    ext_ref[HALO:HALO + TM, pool_w:pool_w + conv_w] = cc * uc
    ext = ext_ref[...]
    ext_ref[0:HALO, :] = ext[TM:TM + HALO, :]
    e_p = ext[:, 0:pool_w]
    e_c = ext[:, pool_w:pool_w + conv_w]

    s2 = e_p + pltpu.roll(e_p, 1, 0)
    s4 = s2 + pltpu.roll(s2, 2, 0)
    s8 = s4 + pltpu.roll(s4, 4, 0)
    s16 = s8 + pltpu.roll(s8, 8, 0)
    gdim = pool_w // len(POOL_WINDOWS)
    lane = lax.broadcasted_iota(jnp.int32, (1, pool_w), 1)
    wsum = jnp.where(lane < gdim, s2, jnp.where(lane < 2 * gdim, s4, jnp.where(lane < 3 * gdim, s8, s16)))
    win = jnp.where(lane < gdim, POOL_WINDOWS[0],
                    jnp.where(lane < 2 * gdim, POOL_WINDOWS[1],
                              jnp.where(lane < 3 * gdim, POOL_WINDOWS[2], POOL_WINDOWS[3])))
    t = si * TM + lax.broadcasted_iota(jnp.int32, (TM, pool_w), 0)
    cnt = jnp.minimum(t + 1, win).astype(F32)
    pooled = wsum[HALO:, :] / cnt - up
    yp = _dot(pooled.astype(BF16), pwbd_ref[...]) * pscale_ref[...]
    yp = _dot(yp.astype(BF16), ppool_ref[...])

    cw = convw_ref[...]
    conv = cw[0:1, :] * pltpu.roll(e_c, 2, 0) + cw[1:2, :] * pltpu.roll(e_c, 1, 0) + cw[2:3, :] * e_c
    yc = bc * conv[HALO:, :]
    yc = _dot(yc.astype(BF16), pconv_ref[...])

    g0 = jax.nn.sigmoid(_dot(h, win_ref[:, o_g:o_g + d_model]))
    g1 = jax.nn.sigmoid(_dot(h, win_ref[:, o_g + d_model:o_g + 2 * d_model]))
    part_ref[0] = (g0 * yp + g1 * yc).astype(BF16)
    ga_ref[0] = jax.nn.sigmoid(_dot(h, win_ref[:, o_g + 2 * d_model:o_g + 3 * d_model])).astype(BF16)

    rc = rc_ref[0]
    rsa = rsa_ref[0]
    rsb = rsb_ref[0]

    def norm_rope(u, w_tiled):
        ss = _dot((u * u).astype(BF16), ind_ref[...])
        un = u * lax.rsqrt(ss + EPS) * w_tiled
        outs = []
        for p in range(attn_w // PAIR):
            cp = un[:, p * PAIR:(p + 1) * PAIR]
            outs.append(cp * rc + pltpu.roll(cp, PAIR - ROPE_HALF, 1) * rsa + pltpu.roll(cp, ROPE_HALF, 1) * rsb)
        return outs

    uq = _dot(h, win_ref[:, o_q:o_k])
    for p, qp in enumerate(norm_rope(uq, qw_ref[...])):
        qT_ref[0, p * PAIR:(p + 1) * PAIR, :] = (qp * Q_SCALE).T.astype(BF16)

    uk = _dot(h, win_ref[:, o_k:o_v])
    for p, kp in enumerate(norm_rope(uk, kw_ref[...])):
        k_ref[0, :, p * PAIR:(p + 1) * PAIR] = kp.astype(BF16)
        for j in range(TM // MOBA_BLOCK):
            km_ref[0, 0, j:j + 1, p * PAIR:(p + 1) * PAIR] = jnp.mean(
                kp[j * MOBA_BLOCK:(j + 1) * MOBA_BLOCK, :], axis=0, keepdims=True)

    uv = _dot(h, win_ref[:, o_v:o_g])
    for p in range(attn_w // PAIR):
        vT_ref[0, p * PAIR:(p + 1) * PAIR, :] = uv[:, p * PAIR:(p + 1) * PAIR].T.astype(BF16)


def _in_call(x, sh1, sc1, n1w, w_in, qw, kw, ind, rc, rsa, rsb, pwbd, pscale, convw, ppool, pconv):
    B, S, D = x.shape
    pool_w = pwbd.shape[0]
    conv_w = convw.shape[1]
    A = ATTN_WIDTH
    nt = S // TM
    row = lambda b, s: (b, s, 0)
    vec = lambda b, s: (b, 0, 0)
    tr = lambda b, s: (b, 0, s)
    kern = functools.partial(_in_kernel, d_model=D, pool_w=pool_w, conv_w=conv_w, attn_w=A)
    return pl.pallas_call(
        kern,
        grid=(B, nt),
        in_specs=[pl.BlockSpec((1, TM, D), row),
                  pl.BlockSpec((1, 1, D), vec), pl.BlockSpec((1, 1, D), vec),
                  _const_spec((1, D)), _const_spec(w_in.shape),
                  _const_spec((1, A)), _const_spec((1, A)), _const_spec((A, A)),
                  pl.BlockSpec((1, TM, LANES), row), pl.BlockSpec((1, TM, LANES), row),
                  pl.BlockSpec((1, TM, LANES), row),
                  _const_spec(pwbd.shape), _const_spec((1, pool_w)), _const_spec(convw.shape),
                  _const_spec(ppool.shape), _const_spec(pconv.shape)],
        out_specs=(pl.BlockSpec((1, TM, D), row), pl.BlockSpec((1, TM, D), row),
                   pl.BlockSpec((1, A, TM), tr), pl.BlockSpec((1, TM, A), row),
                   pl.BlockSpec((1, A, TM), tr),
                   pl.BlockSpec((1, 1, TM // MOBA_BLOCK, A), lambda b, s: (b, s, 0, 0))),
        out_shape=(jax.ShapeDtypeStruct((B, S, D), BF16), jax.ShapeDtypeStruct((B, S, D), BF16),
                   jax.ShapeDtypeStruct((B, A, S), BF16), jax.ShapeDtypeStruct((B, S, A), BF16),
                   jax.ShapeDtypeStruct((B, A, S), BF16),
                   jax.ShapeDtypeStruct((B, nt, TM // MOBA_BLOCK, A), F32)),
        scratch_shapes=[pltpu.VMEM((TM + HALO, pool_w + conv_w), F32)],
        compiler_params=_params(2),
        name="in_proj_mixers",
    )(x, sh1, sc1, n1w, w_in, qw, kw, ind, rc, rsa, rsb, pwbd, pscale, convw, ppool, pconv)


def _attn_kernel(qT_ref, k_ref, vT_ref, km_ref, o_ref, qm_ref, bias_ref, m_ref, l_ref, acc_ref, *, n_blocks):
    qi = pl.program_id(1)
    TQ = MOBA_BLOCK
    heads_per_pair = PAIR // HEAD_DIM
    km = km_ref[0]
    blk = lax.broadcasted_iota(jnp.int32, (n_blocks, TQ), 0)
    past = blk < qi
    kpos = lax.broadcasted_iota(jnp.int32, (TQ, TQ), 0)
    qpos = lax.broadcasted_iota(jnp.int32, (TQ, TQ), 1)
    causal = kpos <= qpos
    half = lax.broadcasted_iota(jnp.int32, (PAIR, TQ), 0) // HEAD_DIM
    own = pl.multiple_of(qi * TQ, TQ)

    def softmax_step(h, s, v_blk, m_i, l_i, acc):
        m_n = jnp.max(s, axis=0, keepdims=True)
        if m_i is not None:
            m_n = jnp.maximum(m_i, m_n)
        pr = jnp.exp2(s - m_n)
        l_n = jnp.sum(pr, axis=0, keepdims=True)
        pv = _dot(v_blk, pr.astype(BF16))
        if m_i is not None:
            alpha = jnp.exp2(m_i - m_n)
            l_n = alpha * l_i + l_n
            pv = alpha * acc + pv
        m_ref[h:h + 1, :] = m_n
        l_ref[h:h + 1, :] = l_n
        acc_ref[h * HEAD_DIM:(h + 1) * HEAD_DIM, :] = pv

    for h in range(ATTN_HEADS):
        p, hh = divmod(h, heads_per_pair)
        lanes = slice(p * PAIR, (p + 1) * PAIR)
        rows = slice(h * HEAD_DIM, (h + 1) * HEAD_DIM)
        qT_pair = qT_ref[0, lanes, :]
        qT_h = jnp.where(half == hh, qT_pair, jnp.zeros_like(qT_pair))
        qm_ref[h * PAIR:(h + 1) * PAIR, :] = qT_h

        km_pair = km[:, lanes]
        km_hi = km_pair.astype(BF16)
        km_lo = (km_pair - km_hi.astype(F32)).astype(BF16)
        gate = _dot(km_hi, qT_h) + _dot(km_lo, qT_h)
        gate = jnp.where(past, gate, NEG)
        beaten = jnp.zeros((n_blocks, TQ), jnp.int32)
        for m in range(n_blocks):
            gm = gate[m:m + 1, :]
            ahead = (gm > gate) | ((gm == gate) & (m < blk))
            beaten = beaten + ahead.astype(jnp.int32)
        sel = past & (beaten < MOBA_TOPK)
        bias_ref[h * n_blocks:(h + 1) * n_blocks, :] = jnp.where(sel, 0.0, NEG)

        s = _dot(k_ref[0, pl.ds(own, TQ), lanes], qT_h)
        s = jnp.where(causal, s, NEG)
        softmax_step(h, s, vT_ref[0, rows, pl.ds(own, TQ)], None, None, None)

    def body(n, carry):
        off = pl.multiple_of(n * TQ, TQ)

        def scores(h):
            p = h // heads_per_pair
            k_n = k_ref[0, pl.ds(off, TQ), p * PAIR:(p + 1) * PAIR]
            return _dot(k_n, qm_ref[h * PAIR:(h + 1) * PAIR, :]) + bias_ref[pl.ds(h * n_blocks + n, 1), :]

        pending = [scores(h) for h in range(QK_AHEAD)]
        for h in range(ATTN_HEADS):
            if h + QK_AHEAD < ATTN_HEADS:
                pending.append(scores(h + QK_AHEAD))
            rows = slice(h * HEAD_DIM, (h + 1) * HEAD_DIM)
            softmax_step(h, pending.pop(0), vT_ref[0, rows, pl.ds(off, TQ)],
                         m_ref[h:h + 1, :], l_ref[h:h + 1, :], acc_ref[rows, :])
        return carry

    lax.fori_loop(0, qi, body, 0)

    for h in range(ATTN_HEADS):
        rows = slice(h * HEAD_DIM, (h + 1) * HEAD_DIM)
        acc_ref[rows, :] = acc_ref[rows, :] / l_ref[h:h + 1, :]
    o_ref[0] = acc_ref[...].T.astype(BF16)


def _attn_call(qT, k, vT, km):
    B, A, S = qT.shape
    nb = S // MOBA_BLOCK
    TQ = MOBA_BLOCK
    kern = functools.partial(_attn_kernel, n_blocks=nb)
    return pl.pallas_call(
        kern,
        grid=(B, nb),
        in_specs=[pl.BlockSpec((1, A, TQ), lambda b, i: (b, 0, i)),
                  pl.BlockSpec((1, S, A), lambda b, i: (b, 0, 0)),
                  pl.BlockSpec((1, A, S), lambda b, i: (b, 0, 0)),
                  pl.BlockSpec((1, nb, A), lambda b, i: (b, 0, 0))],
        out_specs=pl.BlockSpec((1, TQ, A), lambda b, i: (b, i, 0)),
        out_shape=jax.ShapeDtypeStruct((B, S, A), BF16),
        scratch_shapes=[pltpu.VMEM((ATTN_HEADS * PAIR, TQ), BF16),
                        pltpu.VMEM((ATTN_HEADS * nb, TQ), F32),
                        pltpu.VMEM((ATTN_HEADS, TQ), F32),
                        pltpu.VMEM((ATTN_HEADS, TQ), F32),
                        pltpu.VMEM((A, TQ), F32)],
        compiler_params=_params(2),
        name="moba_attention",
    )(qT, k, vT, km)


def _post_kernel(x_ref, part_ref, ga_ref, ya_ref, g1_ref, sh_ref, sc_ref, g2_ref, n2w_ref,
                 pattn_ref, wout_ref, wg_ref, wu_ref, wd_ref, o_ref, act_ref, *, ff_chunk):
    merged = part_ref[0].astype(F32) + ga_ref[0].astype(F32) * _dot(ya_ref[0], pattn_ref[...])
    x1 = x_ref[0] + g1_ref[0] * _dot(merged.astype(BF16), wout_ref[...])
    ms = jnp.mean(x1 * x1, axis=-1, keepdims=True)
    h2 = (x1 * lax.rsqrt(ms + EPS) * n2w_ref[...]) * (1.0 + sc_ref[0]) + sh_ref[0]
    h2 = h2.astype(BF16)
    d_ff = wg_ref.shape[1]
    for j in range(d_ff // ff_chunk):
        cols = slice(j * ff_chunk, (j + 1) * ff_chunk)
        g = _dot(h2, wg_ref[:, cols])
        u = _dot(h2, wu_ref[:, cols])
        act_ref[:, cols] = (g * jax.nn.sigmoid(g) * u).astype(BF16)
    o_ref[0] = x1 + g2_ref[0] * _dot(act_ref[...], wd_ref[...])


def _post_call(x, part, ga, ya, g1, sh2, sc2, g2, n2w, pattn, wout, wg, wu, wd):
    B, S, D = x.shape
    A = ya.shape[-1]
    d_ff = wg.shape[1]
    ff_chunk = 256
    assert d_ff % ff_chunk == 0
    row = lambda b, s: (b, s, 0)
    vec = lambda b, s: (b, 0, 0)
    kern = functools.partial(_post_kernel, ff_chunk=ff_chunk)
    return pl.pallas_call(
        kern,
        grid=(B, S // TM),
        in_specs=[pl.BlockSpec((1, TM, D), row), pl.BlockSpec((1, TM, D), row),
                  pl.BlockSpec((1, TM, D), row), pl.BlockSpec((1, TM, A), row),
                  pl.BlockSpec((1, 1, D), vec), pl.BlockSpec((1, 1, D), vec),
                  pl.BlockSpec((1, 1, D), vec), pl.BlockSpec((1, 1, D), vec),
                  _const_spec((1, D)),
                  _const_spec(pattn.shape), _const_spec(wout.shape), _const_spec(wg.shape),
                  _const_spec(wu.shape), _const_spec(wd.shape)],
        out_specs=pl.BlockSpec((1, TM, D), row),
        out_shape=jax.ShapeDtypeStruct((B, S, D), F32),
        scratch_shapes=[pltpu.VMEM((TM, d_ff), BF16)],
        compiler_params=_params(2),
        name="merge_swiglu",
    )(x, part, ga, ya, g1, sh2, sc2, g2, n2w, pattn, wout, wg, wu, wd)


def kernel(x, c, positions, norm1_w, norm2_w, w_ada, b_ada, w_in, pool_w, pool_scale, conv_w,
           q_norm_w, k_norm_w, p_pool, p_conv, p_attn, w_out, w_gate, w_up, w_down):
    B, S, D = x.shape
    L = w_ada.shape[0]
    assert S % TM == 0 and TM % MOBA_BLOCK == 0 and HALO >= max(POOL_WINDOWS) - 1
    n_groups, gdim = pool_w.shape[1], pool_w.shape[2]
    assert n_groups == len(POOL_WINDOWS) and conv_w.shape[1] == CONV_K

    mod = _modulation(c, w_ada, b_ada).reshape(L, B, 6, 1, D)
    rc, rsa, rsb = _rope_tables(positions)

    head = np.arange(ATTN_WIDTH) // HEAD_DIM
    ind = jnp.asarray((head[:, None] == head[None, :]) / HEAD_DIM, BF16)
    heads_per_w = ATTN_WIDTH // HEAD_DIM

    for l in range(L):
        pwbd = jnp.zeros((n_groups * gdim, n_groups * gdim), F32)
        for g in range(n_groups):
            pwbd = pwbd.at[g * gdim:(g + 1) * gdim, g * gdim:(g + 1) * gdim].set(pool_w[l, g])
        part, ga, qT, k, vT, km = _in_call(
            x, mod[l, :, 0], mod[l, :, 1], norm1_w[l].reshape(1, D), w_in[l].astype(BF16),
            jnp.tile(q_norm_w[l], heads_per_w).reshape(1, ATTN_WIDTH),
            jnp.tile(k_norm_w[l], heads_per_w).reshape(1, ATTN_WIDTH), ind,
            rc, rsa, rsb, pwbd.astype(BF16), pool_scale[l].reshape(1, -1), conv_w[l],
            p_pool[l].astype(BF16), p_conv[l].astype(BF16))
        ya = _attn_call(qT, k, vT, km.reshape(B, S // MOBA_BLOCK, ATTN_WIDTH))
        x = _post_call(x, part, ga, ya, mod[l, :, 2], mod[l, :, 3], mod[l, :, 4], mod[l, :, 5],
                       norm2_w[l].reshape(1, D), p_attn[l].astype(BF16), w_out[l].astype(BF16),
                       w_gate[l].astype(BF16), w_up[l].astype(BF16), w_down[l].astype(BF16))
    return x
```

```python
import functools
import math

import numpy as np
import jax
import jax.numpy as jnp
from jax import lax
from jax.experimental import pallas as pl
from jax.experimental.pallas import tpu as pltpu

F32 = jnp.float32
BF16 = jnp.bfloat16

HEAD_DIM = 64
ATTN_HEADS = 8
ATTN_WIDTH = ATTN_HEADS * HEAD_DIM
ROPE_DIM = HEAD_DIM // 4
ROPE_HALF = ROPE_DIM // 2
ROPE_THETA = 500000.0
MOBA_BLOCK = 256
MOBA_TOPK = 3
POOL_WINDOWS = (2, 4, 8, 16)
CONV_K = 3
EPS = 1e-6
NEG = -1e30

LANES = 128
PAIR = LANES
HALO = 16
VMEM_LIMIT = 56 * 1024 * 1024

TM = 512
ADA_TN = 1536
QK_AHEAD = 5
BF16_SUBLANES = 16
V_ROWS = HEAD_DIM + BF16_SUBLANES
Q_SCALE = (1.0 / math.sqrt(HEAD_DIM)) * math.log2(math.e)


def _dot(a, b):
    return jnp.dot(a, b, preferred_element_type=F32)


def _const_spec(shape):
    nd = len(shape)
    return pl.BlockSpec(shape, lambda *_: (0,) * nd, pipeline_mode=pl.Buffered(1))


def _params(n_axes):
    return pltpu.CompilerParams(dimension_semantics=("arbitrary",) * n_axes,
                                vmem_limit_bytes=VMEM_LIMIT)


def _mod_kernel(c_ref, w_ref, b_ref, o_ref):
    c = c_ref[...]
    c_act = (c * jax.nn.sigmoid(c)).astype(BF16)
    o_ref[0] = _dot(c_act, w_ref[0].astype(BF16)) + b_ref[0]


def _modulation(c, w_ada, b_ada):
    L, D, N = w_ada.shape
    B = c.shape[0]
    return pl.pallas_call(
        _mod_kernel,
        grid=(L, N // ADA_TN),
        in_specs=[pl.BlockSpec((B, D), lambda l, j: (0, 0)),
                  pl.BlockSpec((1, D, ADA_TN), lambda l, j: (l, 0, j)),
                  pl.BlockSpec((1, 1, ADA_TN), lambda l, j: (l, 0, j))],
        out_specs=pl.BlockSpec((1, B, ADA_TN), lambda l, j: (l, 0, j)),
        out_shape=jax.ShapeDtypeStruct((L, B, N), F32),
        compiler_params=_params(2),
        name="adaln_modulation",
    )(c, w_ada, b_ada.reshape(L, 1, N))


def _rope_kernel(pos_ref, freq_ref, c_ref, sa_ref, sb_ref):
    ang = pos_ref[0].astype(F32) * freq_ref[...]
    cs = jnp.cos(ang)
    sn = jnp.sin(ang)
    d = lax.broadcasted_iota(jnp.int32, (1, LANES), 1) & (HEAD_DIM - 1)
    c_ref[0] = jnp.where(d < ROPE_DIM, cs, 1.0)
    sa_ref[0] = jnp.where(d < ROPE_HALF, -sn, 0.0)
    sb_ref[0] = jnp.where((d >= ROPE_HALF) & (d < ROPE_DIM), sn, 0.0)


def _rope_tables(positions):
    B, S = positions.shape
    d = np.arange(LANES) % HEAD_DIM
    freq = np.where(d < ROPE_DIM, ROPE_THETA ** (-(2.0 * (d % ROPE_HALF)) / ROPE_DIM), 0.0)
    freq = jnp.asarray(freq.reshape(1, LANES), F32)
    spec = pl.BlockSpec((1, S, LANES), lambda b: (b, 0, 0))
    shp = jax.ShapeDtypeStruct((B, S, LANES), F32)
    return pl.pallas_call(
        _rope_kernel,
        grid=(B,),
        in_specs=[pl.BlockSpec((1, S, 1), lambda b: (b, 0, 0)),
                  pl.BlockSpec((1, LANES), lambda b: (0, 0))],
        out_specs=(spec, spec, spec),
        out_shape=(shp, shp, shp),
        compiler_params=_params(1),
        name="rotary_tables",
    )(positions.reshape(B, S, 1), freq)


def _in_kernel(x_ref, sh_ref, sc_ref, n1w_ref, win_ref, qw_ref, kw_ref, ind_ref,
               rc_ref, rsa_ref, rsb_ref, pwbd_ref, pscale_ref, convw_ref, ppool_ref, pconv_ref,
               part_ref, ga_ref, qT_ref, kx_ref, vx_ref, km_ref, ext_ref,
               *, d_model, pool_w, conv_w, attn_w):
    si = pl.program_id(1)
    x = x_ref[0]
    ms = jnp.mean(x * x, axis=-1, keepdims=True)
    h = (x * lax.rsqrt(ms + EPS)) * (n1w_ref[...] * (1.0 + sc_ref[0])) + sh_ref[0]
    h = h.astype(BF16)

    @pl.when(si == 0)
    def _():
        ext_ref[0:HALO, :] = jnp.zeros((HALO, pool_w + conv_w), F32)

    o_pool = 0
    o_uc = o_pool + pool_w
    o_bc = o_uc + conv_w
    o_cc = o_bc + conv_w
    o_q = o_cc + conv_w
    o_k = o_q + attn_w
    o_v = o_k + attn_w
    o_g = o_v + attn_w
    n_pairs = attn_w // PAIR

    rc = rc_ref[0]
    rsa = rsa_ref[0]
    rsb = rsb_ref[0]

    def norm_rope(u, w_tiled):
        ss = _dot((u * u).astype(BF16), ind_ref[...])
        un = u * lax.rsqrt(ss + EPS) * w_tiled
        outs = []
        for p in range(n_pairs):
            cp = un[:, p * PAIR:(p + 1) * PAIR]
            outs.append(cp * rc + pltpu.roll(cp, PAIR - ROPE_HALF, 1) * rsa + pltpu.roll(cp, ROPE_HALF, 1) * rsb)
        return outs

    uq = _dot(h, win_ref[:, o_q:o_k])
    for p, qp in enumerate(norm_rope(uq, qw_ref[...])):
        qT_ref[0, p * PAIR:(p + 1) * PAIR, :] = (qp * Q_SCALE).T.astype(BF16)

    row_blk = si * (TM // MOBA_BLOCK) + lax.broadcasted_iota(jnp.int32, (TM, PAIR), 0) // MOBA_BLOCK
    onehot = jnp.where(lax.broadcasted_iota(jnp.int32, (TM, PAIR), 1) == row_blk, 1.0, 0.0).astype(BF16)
    uk = _dot(h, win_ref[:, o_k:o_v])
    for p, kp in enumerate(norm_rope(uk, kw_ref[...])):
        kx_ref[0, :, 2 * p * PAIR:(2 * p + 1) * PAIR] = kp.astype(BF16)
        kx_ref[0, :, (2 * p + 1) * PAIR:(2 * p + 2) * PAIR] = onehot
        for j in range(TM // MOBA_BLOCK):
            km_ref[0, 0, j:j + 1, p * PAIR:(p + 1) * PAIR] = jnp.mean(
                kp[j * MOBA_BLOCK:(j + 1) * MOBA_BLOCK, :], axis=0, keepdims=True)

    ones_rows = jnp.where(lax.broadcasted_iota(jnp.int32, (V_ROWS - HEAD_DIM, TM), 0) == 0, 1.0, 0.0).astype(BF16)
    uv = _dot(h, win_ref[:, o_v:o_g])
    for p in range(n_pairs):
        vT = uv[:, p * PAIR:(p + 1) * PAIR].T.astype(BF16)
        for hh in range(PAIR // HEAD_DIM):
            r0 = (p * (PAIR // HEAD_DIM) + hh) * V_ROWS
            vx_ref[0, r0:r0 + HEAD_DIM, :] = vT[hh * HEAD_DIM:(hh + 1) * HEAD_DIM, :]
            vx_ref[0, r0 + HEAD_DIM:r0 + V_ROWS, :] = ones_rows

    ua = _dot(h, win_ref[:, o_pool:o_q])
    up = ua[:, o_pool:o_uc]
    uc = ua[:, o_uc:o_bc]
    bc = ua[:, o_bc:o_cc]
    cc = ua[:, o_cc:o_q]
    ext_ref[HALO:HALO + TM, 0:pool_w] = up
    ext_ref[HALO:HALO + TM, pool_w:pool_w + conv_w] = cc * uc
    ext = ext_ref[...]
    ext_ref[0:HALO, :] = ext[TM:TM + HALO, :]
    e_p = ext[:, 0:pool_w]
    e_c = ext[:, pool_w:pool_w + conv_w]

    s2 = e_p + pltpu.roll(e_p, 1, 0)
    s4 = s2 + pltpu.roll(s2, 2, 0)
    s8 = s4 + pltpu.roll(s4, 4, 0)
    s16 = s8 + pltpu.roll(s8, 8, 0)
    gdim = pool_w // len(POOL_WINDOWS)
    lane = lax.broadcasted_iota(jnp.int32, (1, pool_w), 1)
    wsum = jnp.where(lane < gdim, s2, jnp.where(lane < 2 * gdim, s4, jnp.where(lane < 3 * gdim, s8, s16)))
    win = jnp.where(lane < gdim, POOL_WINDOWS[0],
                    jnp.where(lane < 2 * gdim, POOL_WINDOWS[1],
                              jnp.where(lane < 3 * gdim, POOL_WINDOWS[2], POOL_WINDOWS[3])))
    t = si * TM + lax.broadcasted_iota(jnp.int32, (TM, pool_w), 0)
    cnt = jnp.minimum(t + 1, win).astype(F32)
    pooled = wsum[HALO:, :] / cnt - up
    yp = _dot(pooled.astype(BF16), pwbd_ref[...]) * pscale_ref[...]
    yp = _dot(yp.astype(BF16), ppool_ref[...])

    cw = convw_ref[...]
    conv = cw[0:1, :] * pltpu.roll(e_c, 2, 0) + cw[1:2, :] * pltpu.roll(e_c, 1, 0) + cw[2:3, :] * e_c
    yc = bc * conv[HALO:, :]
    yc = _dot(yc.astype(BF16), pconv_ref[...])

    g0 = jax.nn.sigmoid(_dot(h, win_ref[:, o_g:o_g + d_model]))
    g1 = jax.nn.sigmoid(_dot(h, win_ref[:, o_g + d_model:o_g + 2 * d_model]))
    part_ref[0] = (g0 * yp + g1 * yc).astype(BF16)
    ga_ref[0] = jax.nn.sigmoid(_dot(h, win_ref[:, o_g + 2 * d_model:o_g + 3 * d_model])).astype(BF16)


def _in_call(x, sh1, sc1, n1w, w_in, qw, kw, ind, rc, rsa, rsb, pwbd, pscale, convw, ppool, pconv):
    B, S, D = x.shape
    pool_w = pwbd.shape[0]
    conv_w = convw.shape[1]
    A = ATTN_WIDTH
    nt = S // TM
    row = lambda b, s: (b, s, 0)
    vec = lambda b, s: (b, 0, 0)
    tr = lambda b, s: (b, 0, s)
    kern = functools.partial(_in_kernel, d_model=D, pool_w=pool_w, conv_w=conv_w, attn_w=A)
    return pl.pallas_call(
        kern,
        grid=(B, nt),
        in_specs=[pl.BlockSpec((1, TM, D), row),
                  pl.BlockSpec((1, 1, D), vec), pl.BlockSpec((1, 1, D), vec),
                  _const_spec((1, D)), _const_spec(w_in.shape),
                  _const_spec((1, A)), _const_spec((1, A)), _const_spec((A, A)),
                  pl.BlockSpec((1, TM, LANES), row), pl.BlockSpec((1, TM, LANES), row),
                  pl.BlockSpec((1, TM, LANES), row),
                  _const_spec(pwbd.shape), _const_spec((1, pool_w)), _const_spec(convw.shape),
                  _const_spec(ppool.shape), _const_spec(pconv.shape)],
        out_specs=(pl.BlockSpec((1, TM, D), row), pl.BlockSpec((1, TM, D), row),
                   pl.BlockSpec((1, A, TM), tr), pl.BlockSpec((1, TM, 2 * A), row),
                   pl.BlockSpec((1, ATTN_HEADS * V_ROWS, TM), tr),
                   pl.BlockSpec((1, 1, TM // MOBA_BLOCK, A), lambda b, s: (b, s, 0, 0))),
        out_shape=(jax.ShapeDtypeStruct((B, S, D), BF16), jax.ShapeDtypeStruct((B, S, D), BF16),
                   jax.ShapeDtypeStruct((B, A, S), BF16), jax.ShapeDtypeStruct((B, S, 2 * A), BF16),
                   jax.ShapeDtypeStruct((B, ATTN_HEADS * V_ROWS, S), BF16),
                   jax.ShapeDtypeStruct((B, nt, TM // MOBA_BLOCK, A), F32)),
        scratch_shapes=[pltpu.VMEM((TM + HALO, pool_w + conv_w), F32)],
        compiler_params=_params(2),
        name="in_proj_mixers",
    )(x, sh1, sc1, n1w, w_in, qw, kw, ind, rc, rsa, rsb, pwbd, pscale, convw, ppool, pconv)


def _attn_kernel(qT_ref, kx_ref, vx_ref, km_ref, o_ref, *, n_blocks):
    TQ = MOBA_BLOCK
    heads_per_pair = PAIR // HEAD_DIM
    km = km_ref[0]
    blk = lax.broadcasted_iota(jnp.int32, (n_blocks, TQ), 0)
    kpos = lax.broadcasted_iota(jnp.int32, (TQ, TQ), 0)
    qpos = lax.broadcasted_iota(jnp.int32, (TQ, TQ), 1)
    causal = kpos <= qpos
    half = lax.broadcasted_iota(jnp.int32, (PAIR, TQ), 0) // HEAD_DIM
    bias_pad = jnp.zeros((BF16_SUBLANES - n_blocks, TQ), F32)
    rhs_pad = jnp.zeros((PAIR - BF16_SUBLANES, TQ), BF16)

    def make_rhs(qi, h):
        p, hh = divmod(h, heads_per_pair)
        lanes = slice(p * PAIR, (p + 1) * PAIR)
        qT_pair = qT_ref[0, lanes, qi * TQ:(qi + 1) * TQ]
        qT_h = jnp.where(half == hh, qT_pair, jnp.zeros_like(qT_pair))
        if qi <= MOBA_TOPK:
            sel = blk <= qi
        else:
            past = blk < qi
            km_pair = km[:, lanes]
            km_hi = km_pair.astype(BF16)
            km_lo = (km_pair - km_hi.astype(F32)).astype(BF16)
            gate = _dot(km_hi, qT_h) + _dot(km_lo, qT_h)
            gate = jnp.where(past, gate, NEG)
            beaten = jnp.zeros((n_blocks, TQ), jnp.int32)
            for m in range(n_blocks):
                gm = gate[m:m + 1, :]
                ahead = (gm > gate) | ((gm == gate) & (m < blk))
                beaten = beaten + ahead.astype(jnp.int32)
            sel = (past & (beaten < MOBA_TOPK)) | (blk == qi)
        bias = jnp.concatenate([jnp.where(sel, 0.0, NEG), bias_pad], axis=0).astype(BF16)
        return jnp.concatenate([qT_h, bias, rhs_pad], axis=0)

    items = [(qi, n, h) for qi in range(n_blocks) for n in [qi] + list(range(qi)) for h in range(ATTN_HEADS)]
    rhs = {}
    state = {}
    done = {}

    def scores(item):
        qi, n, h = item
        if (qi, h) not in rhs:
            rhs[(qi, h)] = make_rhs(qi, h)
        p = h // heads_per_pair
        return _dot(kx_ref[0, n * TQ:(n + 1) * TQ, 2 * p * PAIR:(2 * p + 2) * PAIR], rhs[(qi, h)])

    pending = [scores(it) for it in items[:QK_AHEAD]]
    for i, (qi, n, h) in enumerate(items):
        if i + QK_AHEAD < len(items):
            pending.append(scores(items[i + QK_AHEAD]))
        s = pending.pop(0)
        if n == qi:
            s = jnp.where(causal, s, NEG)
        m_n = jnp.max(s, axis=0, keepdims=True)
        prev = state.get((qi, h))
        if prev is not None:
            m_n = jnp.maximum(prev[0], m_n)
        pr = jnp.exp2(s - m_n).astype(BF16)
        acc = _dot(vx_ref[0, h * V_ROWS:(h + 1) * V_ROWS, n * TQ:(n + 1) * TQ], pr)
        if prev is not None:
            acc = jnp.exp2(prev[0] - m_n) * prev[1] + acc
        state[(qi, h)] = (m_n, acc)
        last_block = qi - 1 if qi > 0 else 0
        if n == last_block:
            done[h] = acc[0:HEAD_DIM, :] / acc[HEAD_DIM:HEAD_DIM + 1, :]
            del state[(qi, h)], rhs[(qi, h)]
            if h == ATTN_HEADS - 1:
                oT = jnp.concatenate([done[j] for j in range(ATTN_HEADS)], axis=0)
                o_ref[0, qi * TQ:(qi + 1) * TQ, :] = oT.T.astype(BF16)


def _attn_call(qT, kx, vx, km):
    B, A, S = qT.shape
    nb = S // MOBA_BLOCK
    assert nb <= BF16_SUBLANES and nb <= PAIR
    kern = functools.partial(_attn_kernel, n_blocks=nb)
    whole = lambda b: (b, 0, 0)
    return pl.pallas_call(
        kern,
        grid=(B,),
        in_specs=[pl.BlockSpec((1, A, S), whole),
                  pl.BlockSpec((1, S, 2 * A), whole),
                  pl.BlockSpec((1, ATTN_HEADS * V_ROWS, S), whole),
                  pl.BlockSpec((1, nb, A), whole)],
        out_specs=pl.BlockSpec((1, S, A), whole),
        out_shape=jax.ShapeDtypeStruct((B, S, A), BF16),
        compiler_params=_params(1),
        name="moba_attention",
    )(qT, kx, vx, km)


def _post_kernel(x_ref, part_ref, ga_ref, ya_ref, g1_ref, sh_ref, sc_ref, g2_ref, n2w_ref,
                 pattn_ref, wout_ref, wg_ref, wu_ref, wd_ref, o_ref, act_ref, *, ff_chunk):
    merged = part_ref[0].astype(F32) + ga_ref[0].astype(F32) * _dot(ya_ref[0], pattn_ref[...])
    x1 = x_ref[0] + g1_ref[0] * _dot(merged.astype(BF16), wout_ref[...])
    ms = jnp.mean(x1 * x1, axis=-1, keepdims=True)
    h2 = (x1 * lax.rsqrt(ms + EPS) * n2w_ref[...]) * (1.0 + sc_ref[0]) + sh_ref[0]
    h2 = h2.astype(BF16)
    d_ff = wg_ref.shape[1]
    for j in range(d_ff // ff_chunk):
        cols = slice(j * ff_chunk, (j + 1) * ff_chunk)
        g = _dot(h2, wg_ref[:, cols])
        u = _dot(h2, wu_ref[:, cols])
        act_ref[:, cols] = (g * jax.nn.sigmoid(g) * u).astype(BF16)
    o_ref[0] = x1 + g2_ref[0] * _dot(act_ref[...], wd_ref[...])


def _post_call(x, part, ga, ya, g1, sh2, sc2, g2, n2w, pattn, wout, wg, wu, wd):
    B, S, D = x.shape
    A = ya.shape[-1]
    d_ff = wg.shape[1]
    ff_chunk = 256
    assert d_ff % ff_chunk == 0
    row = lambda b, s: (b, s, 0)
    vec = lambda b, s: (b, 0, 0)
    kern = functools.partial(_post_kernel, ff_chunk=ff_chunk)
    return pl.pallas_call(
        kern,
        grid=(B, S // TM),
        in_specs=[pl.BlockSpec((1, TM, D), row), pl.BlockSpec((1, TM, D), row),
                  pl.BlockSpec((1, TM, D), row), pl.BlockSpec((1, TM, A), row),
                  pl.BlockSpec((1, 1, D), vec), pl.BlockSpec((1, 1, D), vec),
                  pl.BlockSpec((1, 1, D), vec), pl.BlockSpec((1, 1, D), vec),
                  _const_spec((1, D)),
                  _const_spec(pattn.shape), _const_spec(wout.shape), _const_spec(wg.shape),
                  _const_spec(wu.shape), _const_spec(wd.shape)],
        out_specs=pl.BlockSpec((1, TM, D), row),
        out_shape=jax.ShapeDtypeStruct((B, S, D), F32),
        scratch_shapes=[pltpu.VMEM((TM, d_ff), BF16)],
        compiler_params=_params(2),
        name="merge_swiglu",
    )(x, part, ga, ya, g1, sh2, sc2, g2, n2w, pattn, wout, wg, wu, wd)


def kernel(x, c, positions, norm1_w, norm2_w, w_ada, b_ada, w_in, pool_w, pool_scale, conv_w,
           q_norm_w, k_norm_w, p_pool, p_conv, p_attn, w_out, w_gate, w_up, w_down):
    B, S, D = x.shape
    L = w_ada.shape[0]
    assert S % TM == 0 and TM % MOBA_BLOCK == 0 and HALO >= max(POOL_WINDOWS) - 1
    n_groups, gdim = pool_w.shape[1], pool_w.shape[2]
    assert n_groups == len(POOL_WINDOWS) and conv_w.shape[1] == CONV_K

    mod = _modulation(c, w_ada, b_ada).reshape(L, B, 6, 1, D)
    rc, rsa, rsb = _rope_tables(positions)

    head = np.arange(ATTN_WIDTH) // HEAD_DIM
    ind = jnp.asarray((head[:, None] == head[None, :]) / HEAD_DIM, BF16)
    heads_per_w = ATTN_WIDTH // HEAD_DIM

    for l in range(L):
        pwbd = jnp.zeros((n_groups * gdim, n_groups * gdim), F32)
        for g in range(n_groups):
            pwbd = pwbd.at[g * gdim:(g + 1) * gdim, g * gdim:(g + 1) * gdim].set(pool_w[l, g])
        part, ga, qT, k, vT, km = _in_call(
            x, mod[l, :, 0], mod[l, :, 1], norm1_w[l].reshape(1, D), w_in[l].astype(BF16),
            jnp.tile(q_norm_w[l], heads_per_w).reshape(1, ATTN_WIDTH),
            jnp.tile(k_norm_w[l], heads_per_w).reshape(1, ATTN_WIDTH), ind,
            rc, rsa, rsb, pwbd.astype(BF16), pool_scale[l].reshape(1, -1), conv_w[l],
            p_pool[l].astype(BF16), p_conv[l].astype(BF16))
        ya = _attn_call(qT, k, vT, km.reshape(B, S // MOBA_BLOCK, ATTN_WIDTH))
        x = _post_call(x, part, ga, ya, mod[l, :, 2], mod[l, :, 3], mod[l, :, 4], mod[l, :, 5],
                       norm2_w[l].reshape(1, D), p_attn[l].astype(BF16), w_out[l].astype(BF16),
                       w_gate[l].astype(BF16), w_up[l].astype(BF16), w_down[l].astype(BF16))
    return x
```

```python
import functools
import math

import numpy as np
import jax
import jax.numpy as jnp
from jax import lax
from jax.experimental import pallas as pl
from jax.experimental.pallas import tpu as pltpu

F32 = jnp.float32
BF16 = jnp.bfloat16

HEAD_DIM = 64
ATTN_HEADS = 8
ATTN_WIDTH = ATTN_HEADS * HEAD_DIM
ROPE_DIM = HEAD_DIM // 4
ROPE_HALF = ROPE_DIM // 2
ROPE_THETA = 500000.0
MOBA_BLOCK = 256
MOBA_TOPK = 3
POOL_WINDOWS = (2, 4, 8, 16)
CONV_K = 3
EPS = 1e-6
NEG = -1e30

LANES = 128
PAIR = LANES
HALO = 16
VMEM_LIMIT = 56 * 1024 * 1024

TM = 512
ADA_TN = 1536
FF_TN = 256
GATE_TN = 512
QK_AHEAD = 5
BF16_SUBLANES = 16
V_ROWS = HEAD_DIM + BF16_SUBLANES
Q_SCALE = (1.0 / math.sqrt(HEAD_DIM)) * math.log2(math.e)


def _dot(a, b):
    return jnp.dot(a, b, preferred_element_type=F32)


def _const_spec(shape):
    nd = len(shape)
    return pl.BlockSpec(shape, lambda *_: (0,) * nd, pipeline_mode=pl.Buffered(1))


def _layer_spec(layer, shape):
    nd = len(shape)
    return pl.BlockSpec((None,) + tuple(shape[1:]), lambda *_: (layer,) + (0,) * (nd - 1),
                        pipeline_mode=pl.Buffered(1))


def _params(n_axes):
    return pltpu.CompilerParams(dimension_semantics=("arbitrary",) * n_axes,
                                vmem_limit_bytes=VMEM_LIMIT)


def _mod_kernel(c_ref, w_ref, b_ref, o_ref):
    c = c_ref[...]
    c_act = (c * jax.nn.sigmoid(c)).astype(BF16)
    o_ref[0] = _dot(c_act, w_ref[0].astype(BF16)) + b_ref[0]


def _modulation(c, w_ada, b_ada):
    L, D, N = w_ada.shape
    B = c.shape[0]
    return pl.pallas_call(
        _mod_kernel,
        grid=(L, N // ADA_TN),
        in_specs=[pl.BlockSpec((B, D), lambda l, j: (0, 0)),
                  pl.BlockSpec((1, D, ADA_TN), lambda l, j: (l, 0, j)),
                  pl.BlockSpec((1, 1, ADA_TN), lambda l, j: (l, 0, j))],
        out_specs=pl.BlockSpec((1, B, ADA_TN), lambda l, j: (l, 0, j)),
        out_shape=jax.ShapeDtypeStruct((L, B, N), F32),
        compiler_params=_params(2),
        name="adaln_modulation",
    )(c, w_ada, b_ada.reshape(L, 1, N))


def _rope_kernel(pos_ref, freq_ref, c_ref, sa_ref, sb_ref):
    ang = pos_ref[0].astype(F32) * freq_ref[...]
    cs = jnp.cos(ang)
    sn = jnp.sin(ang)
    d = lax.broadcasted_iota(jnp.int32, (1, LANES), 1) & (HEAD_DIM - 1)
    c_ref[0] = jnp.where(d < ROPE_DIM, cs, 1.0)
    sa_ref[0] = jnp.where(d < ROPE_HALF, -sn, 0.0)
    sb_ref[0] = jnp.where((d >= ROPE_HALF) & (d < ROPE_DIM), sn, 0.0)


def _rope_tables(positions):
    B, S = positions.shape
    d = np.arange(LANES) % HEAD_DIM
    freq = np.where(d < ROPE_DIM, ROPE_THETA ** (-(2.0 * (d % ROPE_HALF)) / ROPE_DIM), 0.0)
    freq = jnp.asarray(freq.reshape(1, LANES), F32)
    spec = pl.BlockSpec((1, S, LANES), lambda b: (b, 0, 0))
    shp = jax.ShapeDtypeStruct((B, S, LANES), F32)
    return pl.pallas_call(
        _rope_kernel,
        grid=(B,),
        in_specs=[pl.BlockSpec((1, S, 1), lambda b: (b, 0, 0)),
                  pl.BlockSpec((1, LANES), lambda b: (0, 0))],
        out_specs=(spec, spec, spec),
        out_shape=(shp, shp, shp),
        compiler_params=_params(1),
        name="rotary_tables",
    )(positions.reshape(B, S, 1), freq)


def _in_kernel(x_ref, mod_ref, n1w_ref, win_ref, qw_ref, kw_ref, ind_ref,
               rc_ref, rsa_ref, rsb_ref, pwbd_ref, pscale_ref, convw_ref, ppool_ref, pconv_ref,
               part_ref, ga_ref, qT_ref, kx_ref, vx_ref, km_ref, ext_ref,
               *, d_model, pool_w, conv_w, attn_w):
    si = pl.program_id(1)
    x = x_ref[0]
    ms = jnp.mean(x * x, axis=-1, keepdims=True)
    sh1 = mod_ref[0:1, :]
    sc1 = mod_ref[1:2, :]
    h = (x * lax.rsqrt(ms + EPS)) * (n1w_ref[...] * (1.0 + sc1)) + sh1
    h = h.astype(BF16)

    @pl.when(si == 0)
    def _():
        ext_ref[0:HALO, :] = jnp.zeros((HALO, pool_w + conv_w), F32)

    o_pool = 0
    o_uc = o_pool + pool_w
    o_bc = o_uc + conv_w
    o_cc = o_bc + conv_w
    o_q = o_cc + conv_w
    o_k = o_q + attn_w
    o_v = o_k + attn_w
    o_g = o_v + attn_w
    n_pairs = attn_w // PAIR

    rc = rc_ref[0]
    rsa = rsa_ref[0]
    rsb = rsb_ref[0]

    def norm_rope(u, w_tiled):
        ss = _dot((u * u).astype(BF16), ind_ref[...])
        un = u * lax.rsqrt(ss + EPS) * w_tiled
        outs = []
        for p in range(n_pairs):
            cp = un[:, p * PAIR:(p + 1) * PAIR]
            outs.append(cp * rc + pltpu.roll(cp, PAIR - ROPE_HALF, 1) * rsa + pltpu.roll(cp, ROPE_HALF, 1) * rsb)
        return outs

    def proj(lo, hi):
        return lambda: (_dot(h, win_ref[:, lo:hi]),)

    def q_stage(uq):
        for p, qp in enumerate(norm_rope(uq, qw_ref[...])):
            qT_ref[0, p * PAIR:(p + 1) * PAIR, :] = (qp * Q_SCALE).T.astype(BF16)

    def k_stage(uk):
        row_blk = si * (TM // MOBA_BLOCK) + lax.broadcasted_iota(jnp.int32, (TM, PAIR), 0) // MOBA_BLOCK
        onehot = jnp.where(lax.broadcasted_iota(jnp.int32, (TM, PAIR), 1) == row_blk, 1.0, 0.0).astype(BF16)
        for p, kp in enumerate(norm_rope(uk, kw_ref[...])):
            kx_ref[0, :, 2 * p * PAIR:(2 * p + 1) * PAIR] = kp.astype(BF16)
            kx_ref[0, :, (2 * p + 1) * PAIR:(2 * p + 2) * PAIR] = onehot
            for j in range(TM // MOBA_BLOCK):
                km_ref[0, 0, j:j + 1, p * PAIR:(p + 1) * PAIR] = jnp.mean(
                    kp[j * MOBA_BLOCK:(j + 1) * MOBA_BLOCK, :], axis=0, keepdims=True)

    def v_stage(uv):
        ones_rows = jnp.where(lax.broadcasted_iota(jnp.int32, (V_ROWS - HEAD_DIM, TM), 0) == 0, 1.0, 0.0).astype(BF16)
        for p in range(n_pairs):
            vT = uv[:, p * PAIR:(p + 1) * PAIR].T.astype(BF16)
            for hh in range(PAIR // HEAD_DIM):
                r0 = (p * (PAIR // HEAD_DIM) + hh) * V_ROWS
                vx_ref[0, r0:r0 + HEAD_DIM, :] = vT[hh * HEAD_DIM:(hh + 1) * HEAD_DIM, :]
                vx_ref[0, r0 + HEAD_DIM:r0 + V_ROWS, :] = ones_rows

    mixed = {}

    def mixer_stage(ua):
        up = ua[:, o_pool:o_uc]
        uc = ua[:, o_uc:o_bc]
        bc = ua[:, o_bc:o_cc]
        cc = ua[:, o_cc:o_q]
        ext_ref[HALO:HALO + TM, 0:pool_w] = up
        ext_ref[HALO:HALO + TM, pool_w:pool_w + conv_w] = cc * uc
        ext = ext_ref[...]
        ext_ref[0:HALO, :] = ext[TM:TM + HALO, :]
        e_p = ext[:, 0:pool_w]
        e_c = ext[:, pool_w:pool_w + conv_w]

        s2 = e_p + pltpu.roll(e_p, 1, 0)
        s4 = s2 + pltpu.roll(s2, 2, 0)
        s8 = s4 + pltpu.roll(s4, 4, 0)
        s16 = s8 + pltpu.roll(s8, 8, 0)
        gdim = pool_w // len(POOL_WINDOWS)
        lane = lax.broadcasted_iota(jnp.int32, (1, pool_w), 1)
        wsum = jnp.where(lane < gdim, s2, jnp.where(lane < 2 * gdim, s4, jnp.where(lane < 3 * gdim, s8, s16)))
        win = jnp.where(lane < gdim, POOL_WINDOWS[0],
                        jnp.where(lane < 2 * gdim, POOL_WINDOWS[1],
                                  jnp.where(lane < 3 * gdim, POOL_WINDOWS[2], POOL_WINDOWS[3])))
        t = si * TM + lax.broadcasted_iota(jnp.int32, (TM, pool_w), 0)
        cnt = jnp.minimum(t + 1, win).astype(F32)
        pooled = wsum[HALO:, :] / cnt - up
        yp = _dot(pooled.astype(BF16), pwbd_ref[...]) * pscale_ref[...]
        mixed["pool"] = _dot(yp.astype(BF16), ppool_ref[...])

        cw = convw_ref[...]
        conv = cw[0:1, :] * pltpu.roll(e_c, 2, 0) + cw[1:2, :] * pltpu.roll(e_c, 1, 0) + cw[2:3, :] * e_c
        yc = bc * conv[HALO:, :]
        mixed["conv"] = _dot(yc.astype(BF16), pconv_ref[...])

    def gate_dots(c):
        cols = [slice(o_g + i * d_model + c * GATE_TN, o_g + i * d_model + (c + 1) * GATE_TN) for i in range(3)]
        return lambda: tuple(_dot(h, win_ref[:, cs]) for cs in cols)

    def gate_stage(c):
        out = slice(c * GATE_TN, (c + 1) * GATE_TN)

        def run(g0, g1, g2):
            merged = jax.nn.sigmoid(g0) * mixed["pool"][:, out] + jax.nn.sigmoid(g1) * mixed["conv"][:, out]
            part_ref[0, :, out] = merged.astype(BF16)
            ga_ref[0, :, out] = jax.nn.sigmoid(g2).astype(BF16)
        return run

    stages = [(proj(o_q, o_k), q_stage), (proj(o_k, o_v), k_stage), (proj(o_v, o_g), v_stage),
              (proj(o_pool, o_q), mixer_stage)]
    stages += [(gate_dots(c), gate_stage(c)) for c in range(d_model // GATE_TN)]
    res = stages[0][0]()
    for i, (_, finish) in enumerate(stages):
        nxt = stages[i + 1][0]() if i + 1 < len(stages) else None
        finish(*res)
        res = nxt


def _in_call(layer, x, mod, n1w, w_in, qw, kw, ind, rc, rsa, rsb, pwbd, pscale, convw, ppool, pconv):
    B, S, D = x.shape
    pool_w = pwbd.shape[-1]
    conv_w = convw.shape[-1]
    A = ATTN_WIDTH
    nt = S // TM
    row = lambda b, s: (b, s, 0)
    tr = lambda b, s: (b, 0, s)
    lspec = functools.partial(_layer_spec, layer)
    kern = functools.partial(_in_kernel, d_model=D, pool_w=pool_w, conv_w=conv_w, attn_w=A)
    return pl.pallas_call(
        kern,
        grid=(B, nt),
        in_specs=[pl.BlockSpec((1, TM, D), row),
                  pl.BlockSpec((None, None) + mod.shape[2:], lambda b, s: (layer, b, 0, 0)),
                  lspec(n1w.shape), lspec(w_in.shape),
                  lspec(qw.shape), lspec(kw.shape), _const_spec((A, A)),
                  pl.BlockSpec((1, TM, LANES), row), pl.BlockSpec((1, TM, LANES), row),
                  pl.BlockSpec((1, TM, LANES), row),
                  lspec(pwbd.shape), lspec(pscale.shape), lspec(convw.shape),
                  lspec(ppool.shape), lspec(pconv.shape)],
        out_specs=(pl.BlockSpec((1, TM, D), row), pl.BlockSpec((1, TM, D), row),
                   pl.BlockSpec((1, A, TM), tr), pl.BlockSpec((1, TM, 2 * A), row),
                   pl.BlockSpec((1, ATTN_HEADS * V_ROWS, TM), tr),
                   pl.BlockSpec((1, 1, TM // MOBA_BLOCK, A), lambda b, s: (b, s, 0, 0))),
        out_shape=(jax.ShapeDtypeStruct((B, S, D), BF16), jax.ShapeDtypeStruct((B, S, D), BF16),
                   jax.ShapeDtypeStruct((B, A, S), BF16), jax.ShapeDtypeStruct((B, S, 2 * A), BF16),
                   jax.ShapeDtypeStruct((B, ATTN_HEADS * V_ROWS, S), BF16),
                   jax.ShapeDtypeStruct((B, nt, TM // MOBA_BLOCK, A), F32)),
        scratch_shapes=[pltpu.VMEM((TM + HALO, pool_w + conv_w), F32)],
        compiler_params=_params(2),
        name="in_proj_mixers",
    )(x, mod, n1w, w_in, qw, kw, ind, rc, rsa, rsb, pwbd, pscale, convw, ppool, pconv)


def _attn_kernel(qT_ref, kx_ref, vx_ref, km_ref, o_ref, *, n_blocks):
    TQ = MOBA_BLOCK
    heads_per_pair = PAIR // HEAD_DIM
    km = km_ref[0]
    blk = lax.broadcasted_iota(jnp.int32, (n_blocks, TQ), 0)
    kpos = lax.broadcasted_iota(jnp.int32, (TQ, TQ), 0)
    qpos = lax.broadcasted_iota(jnp.int32, (TQ, TQ), 1)
    causal = kpos <= qpos
    half = lax.broadcasted_iota(jnp.int32, (PAIR, TQ), 0) // HEAD_DIM
    bias_pad = jnp.zeros((BF16_SUBLANES - n_blocks, TQ), F32)
    rhs_pad = jnp.zeros((PAIR - BF16_SUBLANES, TQ), BF16)

    def make_rhs(qi, h):
        p, hh = divmod(h, heads_per_pair)
        lanes = slice(p * PAIR, (p + 1) * PAIR)
        qT_pair = qT_ref[0, lanes, qi * TQ:(qi + 1) * TQ]
        qT_h = jnp.where(half == hh, qT_pair, jnp.zeros_like(qT_pair))
        if qi <= MOBA_TOPK:
            sel = blk <= qi
        else:
            past = blk < qi
            km_pair = km[:, lanes]
            km_hi = km_pair.astype(BF16)
            km_lo = (km_pair - km_hi.astype(F32)).astype(BF16)
            gate = _dot(km_hi, qT_h) + _dot(km_lo, qT_h)
            gate = jnp.where(past, gate, NEG)
            beaten = jnp.zeros((n_blocks, TQ), jnp.int32)
            for m in range(n_blocks):
                gm = gate[m:m + 1, :]
                ahead = (gm > gate) | ((gm == gate) & (m < blk))
                beaten = beaten + ahead.astype(jnp.int32)
            sel = (past & (beaten < MOBA_TOPK)) | (blk == qi)
        bias = jnp.concatenate([jnp.where(sel, 0.0, NEG), bias_pad], axis=0).astype(BF16)
        return jnp.concatenate([qT_h, bias, rhs_pad], axis=0)

    items = [(qi, n, h) for qi in range(n_blocks) for n in [qi] + list(range(qi)) for h in range(ATTN_HEADS)]
    rhs = {}
    state = {}
    done = {}

    def scores(item):
        qi, n, h = item
        if (qi, h) not in rhs:
            rhs[(qi, h)] = make_rhs(qi, h)
        p = h // heads_per_pair
        return _dot(kx_ref[0, n * TQ:(n + 1) * TQ, 2 * p * PAIR:(2 * p + 2) * PAIR], rhs[(qi, h)])

    pending = [scores(it) for it in items[:QK_AHEAD]]
    for i, (qi, n, h) in enumerate(items):
        if i + QK_AHEAD < len(items):
            pending.append(scores(items[i + QK_AHEAD]))
        s = pending.pop(0)
        if n == qi:
            s = jnp.where(causal, s, NEG)
        m_n = jnp.max(s, axis=0, keepdims=True)
        prev = state.get((qi, h))
        if prev is not None:
            m_n = jnp.maximum(prev[0], m_n)
        pr = jnp.exp2(s - m_n).astype(BF16)
        acc = _dot(vx_ref[0, h * V_ROWS:(h + 1) * V_ROWS, n * TQ:(n + 1) * TQ], pr)
        if prev is not None:
            acc = jnp.exp2(prev[0] - m_n) * prev[1] + acc
        state[(qi, h)] = (m_n, acc)
        last_block = qi - 1 if qi > 0 else 0
        if n == last_block:
            done[h] = acc[0:HEAD_DIM, :] / acc[HEAD_DIM:HEAD_DIM + 1, :]
            del state[(qi, h)], rhs[(qi, h)]
            if h == ATTN_HEADS - 1:
                oT = jnp.concatenate([done[j] for j in range(ATTN_HEADS)], axis=0)
                o_ref[0, qi * TQ:(qi + 1) * TQ, :] = oT.T.astype(BF16)


def _attn_call(qT, kx, vx, km):
    B, A, S = qT.shape
    nb = S // MOBA_BLOCK
    assert nb <= BF16_SUBLANES and nb <= PAIR
    kern = functools.partial(_attn_kernel, n_blocks=nb)
    whole = lambda b: (b, 0, 0)
    return pl.pallas_call(
        kern,
        grid=(B,),
        in_specs=[pl.BlockSpec((1, A, S), whole),
                  pl.BlockSpec((1, S, 2 * A), whole),
                  pl.BlockSpec((1, ATTN_HEADS * V_ROWS, S), whole),
                  pl.BlockSpec((1, nb, A), whole)],
        out_specs=pl.BlockSpec((1, S, A), whole),
        out_shape=jax.ShapeDtypeStruct((B, S, A), BF16),
        compiler_params=_params(1),
        name="moba_attention",
    )(qT, kx, vx, km)


def _post_kernel(x_ref, part_ref, ga_ref, ya_ref, mod_ref, n2w_ref,
                 pattn_ref, wout_ref, wg_ref, wu_ref, wd_ref, o_ref, act_ref):
    g1 = mod_ref[2:3, :]
    sh2 = mod_ref[3:4, :]
    sc2 = mod_ref[4:5, :]
    g2 = mod_ref[5:6, :]
    merged = part_ref[0].astype(F32) + ga_ref[0].astype(F32) * _dot(ya_ref[0], pattn_ref[...])
    x1 = x_ref[0] + g1 * _dot(merged.astype(BF16), wout_ref[...])
    ms = jnp.mean(x1 * x1, axis=-1, keepdims=True)
    h2 = (x1 * lax.rsqrt(ms + EPS)) * (n2w_ref[...] * (1.0 + sc2)) + sh2
    h2 = h2.astype(BF16)
    d_ff = wg_ref.shape[1]

    def ff_dots(j):
        cols = slice(j * FF_TN, (j + 1) * FF_TN)
        return _dot(h2, wg_ref[:, cols]), _dot(h2, wu_ref[:, cols])

    res = ff_dots(0)
    for j in range(d_ff // FF_TN):
        nxt = ff_dots(j + 1) if (j + 1) * FF_TN < d_ff else None
        g, u = res
        act_ref[:, j * FF_TN:(j + 1) * FF_TN] = (g * jax.nn.sigmoid(g) * u).astype(BF16)
        res = nxt
    o_ref[0] = x1 + g2 * _dot(act_ref[...], wd_ref[...])


def _post_call(layer, x, part, ga, ya, mod, n2w, pattn, wout, wg, wu, wd):
    B, S, D = x.shape
    A = ya.shape[-1]
    d_ff = wg.shape[-1]
    assert d_ff % FF_TN == 0
    row = lambda b, s: (b, s, 0)
    lspec = functools.partial(_layer_spec, layer)
    return pl.pallas_call(
        _post_kernel,
        grid=(B, S // TM),
        in_specs=[pl.BlockSpec((1, TM, D), row), pl.BlockSpec((1, TM, D), row),
                  pl.BlockSpec((1, TM, D), row), pl.BlockSpec((1, TM, A), row),
                  pl.BlockSpec((None, None) + mod.shape[2:], lambda b, s: (layer, b, 0, 0)),
                  lspec(n2w.shape),
                  lspec(pattn.shape), lspec(wout.shape), lspec(wg.shape), lspec(wu.shape), lspec(wd.shape)],
        out_specs=pl.BlockSpec((1, TM, D), row),
        out_shape=jax.ShapeDtypeStruct((B, S, D), F32),
        scratch_shapes=[pltpu.VMEM((TM, d_ff), BF16)],
        compiler_params=_params(2),
        name="merge_swiglu",
    )(x, part, ga, ya, mod, n2w, pattn, wout, wg, wu, wd)


def kernel(x, c, positions, norm1_w, norm2_w, w_ada, b_ada, w_in, pool_w, pool_scale, conv_w,
           q_norm_w, k_norm_w, p_pool, p_conv, p_attn, w_out, w_gate, w_up, w_down):
    B, S, D = x.shape
    L = w_ada.shape[0]
    assert S % TM == 0 and TM % MOBA_BLOCK == 0 and HALO >= max(POOL_WINDOWS) - 1
    n_groups, gdim = pool_w.shape[1], pool_w.shape[2]
    assert n_groups == len(POOL_WINDOWS) and conv_w.shape[1] == CONV_K

    mod = _modulation(c, w_ada, b_ada).reshape(L, B, 6, D)
    rc, rsa, rsb = _rope_tables(positions)

    head = np.arange(ATTN_WIDTH) // HEAD_DIM
    ind = jnp.asarray((head[:, None] == head[None, :]) / HEAD_DIM, BF16)

    pwbd = jnp.zeros((L, n_groups * gdim, n_groups * gdim), F32)
    for g in range(n_groups):
        pwbd = pwbd.at[:, g * gdim:(g + 1) * gdim, g * gdim:(g + 1) * gdim].set(pool_w[:, g])
    bf = lambda w: w.astype(BF16)
    qw = jnp.tile(q_norm_w, (1, ATTN_HEADS)).reshape(L, 1, ATTN_WIDTH)
    kw = jnp.tile(k_norm_w, (1, ATTN_HEADS)).reshape(L, 1, ATTN_WIDTH)
    in_params = (norm1_w.reshape(L, 1, D), bf(w_in), qw, kw, ind, rc, rsa, rsb, bf(pwbd),
                 pool_scale.reshape(L, 1, -1), conv_w, bf(p_pool), bf(p_conv))
    post_params = (norm2_w.reshape(L, 1, D), bf(p_attn), bf(w_out), bf(w_gate), bf(w_up), bf(w_down))

    for l in range(L):
        part, ga, qT, kx, vx, km = _in_call(l, x, mod, *in_params)
        ya = _attn_call(qT, kx, vx, km.reshape(B, S // MOBA_BLOCK, ATTN_WIDTH))
        x = _post_call(l, x, part, ga, ya, mod, *post_params)
    return x
```

```python
import functools
import math

import numpy as np
import jax
import jax.numpy as jnp
from jax import lax
from jax.experimental import pallas as pl
from jax.experimental.pallas import tpu as pltpu

F32 = jnp.float32
BF16 = jnp.bfloat16

HEAD_DIM = 64
ATTN_HEADS = 8
ATTN_WIDTH = ATTN_HEADS * HEAD_DIM
ROPE_DIM = HEAD_DIM // 4
ROPE_HALF = ROPE_DIM // 2
ROPE_THETA = 500000.0
MOBA_BLOCK = 256
MOBA_TOPK = 3
POOL_WINDOWS = (2, 4, 8, 16)
CONV_K = 3
EPS = 1e-6
NEG = -1e30

LANES = 128
PAIR = LANES
HALO = 16
VMEM_LIMIT = 56 * 1024 * 1024

TM = 512
ADA_TN = 1536
FF_TN = 256
GATE_TN = 512
PV_BEHIND = 1
QK_AHEAD = 5
BF16_SUBLANES = 16
V_ROWS = HEAD_DIM + BF16_SUBLANES
Q_SCALE = (1.0 / math.sqrt(HEAD_DIM)) * math.log2(math.e)


def _dot(a, b):
    return jnp.dot(a, b, preferred_element_type=F32)


def _const_spec(shape):
    nd = len(shape)
    return pl.BlockSpec(shape, lambda *_: (0,) * nd, pipeline_mode=pl.Buffered(1))


def _layer_spec(layer, shape):
    nd = len(shape)
    return pl.BlockSpec((None,) + tuple(shape[1:]), lambda *_: (layer,) + (0,) * (nd - 1),
                        pipeline_mode=pl.Buffered(1))


def _params(n_axes):
    return pltpu.CompilerParams(dimension_semantics=("arbitrary",) * n_axes,
                                vmem_limit_bytes=VMEM_LIMIT)


def _mod_kernel(c_ref, w_ref, b_ref, o_ref):
    c = c_ref[...]
    c_act = (c * jax.nn.sigmoid(c)).astype(BF16)
    o_ref[0] = _dot(c_act, w_ref[0].astype(BF16)) + b_ref[0]


def _modulation(c, w_ada, b_ada):
    L, D, N = w_ada.shape
    B = c.shape[0]
    return pl.pallas_call(
        _mod_kernel,
        grid=(L, N // ADA_TN),
        in_specs=[pl.BlockSpec((B, D), lambda l, j: (0, 0)),
                  pl.BlockSpec((1, D, ADA_TN), lambda l, j: (l, 0, j)),
                  pl.BlockSpec((1, 1, ADA_TN), lambda l, j: (l, 0, j))],
        out_specs=pl.BlockSpec((1, B, ADA_TN), lambda l, j: (l, 0, j)),
        out_shape=jax.ShapeDtypeStruct((L, B, N), F32),
        compiler_params=_params(2),
        name="adaln_modulation",
    )(c, w_ada, b_ada.reshape(L, 1, N))


def _rope_kernel(pos_ref, freq_ref, c_ref, sa_ref, sb_ref):
    ang = pos_ref[0].astype(F32) * freq_ref[...]
    cs = jnp.cos(ang)
    sn = jnp.sin(ang)
    d = lax.broadcasted_iota(jnp.int32, (1, LANES), 1) & (HEAD_DIM - 1)
    c_ref[0] = jnp.where(d < ROPE_DIM, cs, 1.0)
    sa_ref[0] = jnp.where(d < ROPE_HALF, -sn, 0.0)
    sb_ref[0] = jnp.where((d >= ROPE_HALF) & (d < ROPE_DIM), sn, 0.0)


def _rope_tables(positions):
    B, S = positions.shape
    d = np.arange(LANES) % HEAD_DIM
    freq = np.where(d < ROPE_DIM, ROPE_THETA ** (-(2.0 * (d % ROPE_HALF)) / ROPE_DIM), 0.0)
    freq = jnp.asarray(freq.reshape(1, LANES), F32)
    spec = pl.BlockSpec((1, S, LANES), lambda b: (b, 0, 0))
    shp = jax.ShapeDtypeStruct((B, S, LANES), F32)
    return pl.pallas_call(
        _rope_kernel,
        grid=(B,),
        in_specs=[pl.BlockSpec((1, S, 1), lambda b: (b, 0, 0)),
                  pl.BlockSpec((1, LANES), lambda b: (0, 0))],
        out_specs=(spec, spec, spec),
        out_shape=(shp, shp, shp),
        compiler_params=_params(1),
        name="rotary_tables",
    )(positions.reshape(B, S, 1), freq)


def _in_kernel(x_ref, mod_ref, n1w_ref, win_ref, qw_ref, kw_ref, ind_ref,
               rc_ref, rsa_ref, rsb_ref, pwbd_ref, pscale_ref, convw_ref, ppool_ref, pconv_ref,
               part_ref, ga_ref, qT_ref, kx_ref, vx_ref, km_ref, ext_ref,
               *, d_model, pool_w, conv_w, attn_w):
    si = pl.program_id(1)
    x = x_ref[0]
    ms = jnp.mean(x * x, axis=-1, keepdims=True)
    sh1 = mod_ref[0:1, :]
    sc1 = mod_ref[1:2, :]
    h = (x * lax.rsqrt(ms + EPS)) * (n1w_ref[...] * (1.0 + sc1)) + sh1
    h = h.astype(BF16)

    @pl.when(si == 0)
    def _():
        ext_ref[0:HALO, :] = jnp.zeros((HALO, pool_w + conv_w), F32)

    o_pool = 0
    o_uc = o_pool + pool_w
    o_bc = o_uc + conv_w
    o_cc = o_bc + conv_w
    o_q = o_cc + conv_w
    o_k = o_q + attn_w
    o_v = o_k + attn_w
    o_g = o_v + attn_w
    n_pairs = attn_w // PAIR

    rc = rc_ref[0]
    rsa = rsa_ref[0]
    rsb = rsb_ref[0]

    def norm_rope(u, w_tiled):
        ss = _dot((u * u).astype(BF16), ind_ref[...])
        un = u * lax.rsqrt(ss + EPS) * w_tiled
        outs = []
        for p in range(n_pairs):
            cp = un[:, p * PAIR:(p + 1) * PAIR]
            outs.append(cp * rc + pltpu.roll(cp, PAIR - ROPE_HALF, 1) * rsa + pltpu.roll(cp, ROPE_HALF, 1) * rsb)
        return outs

    def proj(lo, hi):
        return lambda: (_dot(h, win_ref[:, lo:hi]),)

    def q_stage(uq):
        for p, qp in enumerate(norm_rope(uq, qw_ref[...])):
            qT_ref[0, p * PAIR:(p + 1) * PAIR, :] = (qp * Q_SCALE).T.astype(BF16)

    def k_stage(uk):
        row_blk = si * (TM // MOBA_BLOCK) + lax.broadcasted_iota(jnp.int32, (TM, PAIR), 0) // MOBA_BLOCK
        onehot = jnp.where(lax.broadcasted_iota(jnp.int32, (TM, PAIR), 1) == row_blk, 1.0, 0.0).astype(BF16)
        for p, kp in enumerate(norm_rope(uk, kw_ref[...])):
            kx_ref[0, :, 2 * p * PAIR:(2 * p + 1) * PAIR] = kp.astype(BF16)
            kx_ref[0, :, (2 * p + 1) * PAIR:(2 * p + 2) * PAIR] = onehot
            for j in range(TM // MOBA_BLOCK):
                km_ref[0, 0, j:j + 1, p * PAIR:(p + 1) * PAIR] = jnp.mean(
                    kp[j * MOBA_BLOCK:(j + 1) * MOBA_BLOCK, :], axis=0, keepdims=True)

    def v_stage(uv):
        ones_rows = jnp.where(lax.broadcasted_iota(jnp.int32, (V_ROWS - HEAD_DIM, TM), 0) == 0, 1.0, 0.0).astype(BF16)
        for p in range(n_pairs):
            vT = uv[:, p * PAIR:(p + 1) * PAIR].T.astype(BF16)
            for hh in range(PAIR // HEAD_DIM):
                r0 = (p * (PAIR // HEAD_DIM) + hh) * V_ROWS
                vx_ref[0, r0:r0 + HEAD_DIM, :] = vT[hh * HEAD_DIM:(hh + 1) * HEAD_DIM, :]
                vx_ref[0, r0 + HEAD_DIM:r0 + V_ROWS, :] = ones_rows

    mixed = {}

    def mixer_stage(ua):
        up = ua[:, o_pool:o_uc]
        uc = ua[:, o_uc:o_bc]
        bc = ua[:, o_bc:o_cc]
        cc = ua[:, o_cc:o_q]
        ext_ref[HALO:HALO + TM, 0:pool_w] = up
        ext_ref[HALO:HALO + TM, pool_w:pool_w + conv_w] = cc * uc
        ext = ext_ref[...]
        ext_ref[0:HALO, :] = ext[TM:TM + HALO, :]
        e_p = ext[:, 0:pool_w]
        e_c = ext[:, pool_w:pool_w + conv_w]

        s2 = e_p + pltpu.roll(e_p, 1, 0)
        s4 = s2 + pltpu.roll(s2, 2, 0)
        s8 = s4 + pltpu.roll(s4, 4, 0)
        s16 = s8 + pltpu.roll(s8, 8, 0)
        gdim = pool_w // len(POOL_WINDOWS)
        lane = lax.broadcasted_iota(jnp.int32, (1, pool_w), 1)
        wsum = jnp.where(lane < gdim, s2, jnp.where(lane < 2 * gdim, s4, jnp.where(lane < 3 * gdim, s8, s16)))
        win = jnp.where(lane < gdim, POOL_WINDOWS[0],
                        jnp.where(lane < 2 * gdim, POOL_WINDOWS[1],
                                  jnp.where(lane < 3 * gdim, POOL_WINDOWS[2], POOL_WINDOWS[3])))
        t = si * TM + lax.broadcasted_iota(jnp.int32, (TM, pool_w), 0)
        cnt = jnp.minimum(t + 1, win).astype(F32)
        pooled = wsum[HALO:, :] / cnt - up
        yp = _dot(pooled.astype(BF16), pwbd_ref[...]) * pscale_ref[...]
        mixed["pool"] = _dot(yp.astype(BF16), ppool_ref[...])

        cw = convw_ref[...]
        conv = cw[0:1, :] * pltpu.roll(e_c, 2, 0) + cw[1:2, :] * pltpu.roll(e_c, 1, 0) + cw[2:3, :] * e_c
        yc = bc * conv[HALO:, :]
        mixed["conv"] = _dot(yc.astype(BF16), pconv_ref[...])

    def gate_dots(c):
        cols = [slice(o_g + i * d_model + c * GATE_TN, o_g + i * d_model + (c + 1) * GATE_TN) for i in range(3)]
        return lambda: tuple(_dot(h, win_ref[:, cs]) for cs in cols)

    def gate_stage(c):
        out = slice(c * GATE_TN, (c + 1) * GATE_TN)

        def run(g0, g1, g2):
            merged = jax.nn.sigmoid(g0) * mixed["pool"][:, out] + jax.nn.sigmoid(g1) * mixed["conv"][:, out]
            part_ref[0, :, out] = merged.astype(BF16)
            ga_ref[0, :, out] = jax.nn.sigmoid(g2).astype(BF16)
        return run

    stages = [(proj(o_q, o_k), q_stage), (proj(o_k, o_v), k_stage), (proj(o_v, o_g), v_stage),
              (proj(o_pool, o_q), mixer_stage)]
    stages += [(gate_dots(c), gate_stage(c)) for c in range(d_model // GATE_TN)]
    res = stages[0][0]()
    for i, (_, finish) in enumerate(stages):
        nxt = stages[i + 1][0]() if i + 1 < len(stages) else None
        finish(*res)
        res = nxt


def _in_call(layer, x, mod, n1w, w_in, qw, kw, ind, rc, rsa, rsb, pwbd, pscale, convw, ppool, pconv):
    B, S, D = x.shape
    pool_w = pwbd.shape[-1]
    conv_w = convw.shape[-1]
    A = ATTN_WIDTH
    nt = S // TM
    row = lambda b, s: (b, s, 0)
    tr = lambda b, s: (b, 0, s)
    lspec = functools.partial(_layer_spec, layer)
    kern = functools.partial(_in_kernel, d_model=D, pool_w=pool_w, conv_w=conv_w, attn_w=A)
    return pl.pallas_call(
        kern,
        grid=(B, nt),
        in_specs=[pl.BlockSpec((1, TM, D), row),
                  pl.BlockSpec((None, None) + mod.shape[2:], lambda b, s: (layer, b, 0, 0)),
                  lspec(n1w.shape), lspec(w_in.shape),
                  lspec(qw.shape), lspec(kw.shape), _const_spec((A, A)),
                  pl.BlockSpec((1, TM, LANES), row), pl.BlockSpec((1, TM, LANES), row),
                  pl.BlockSpec((1, TM, LANES), row),
                  lspec(pwbd.shape), lspec(pscale.shape), lspec(convw.shape),
                  lspec(ppool.shape), lspec(pconv.shape)],
        out_specs=(pl.BlockSpec((1, TM, D), row), pl.BlockSpec((1, TM, D), row),
                   pl.BlockSpec((1, A, TM), tr), pl.BlockSpec((1, TM, 2 * A), row),
                   pl.BlockSpec((1, ATTN_HEADS * V_ROWS, TM), tr),
                   pl.BlockSpec((1, 1, TM // MOBA_BLOCK, A), lambda b, s: (b, s, 0, 0))),
        out_shape=(jax.ShapeDtypeStruct((B, S, D), BF16), jax.ShapeDtypeStruct((B, S, D), BF16),
                   jax.ShapeDtypeStruct((B, A, S), BF16), jax.ShapeDtypeStruct((B, S, 2 * A), BF16),
                   jax.ShapeDtypeStruct((B, ATTN_HEADS * V_ROWS, S), BF16),
                   jax.ShapeDtypeStruct((B, nt, TM // MOBA_BLOCK, A), F32)),
        scratch_shapes=[pltpu.VMEM((TM + HALO, pool_w + conv_w), F32)],
        compiler_params=_params(2),
        name="in_proj_mixers",
    )(x, mod, n1w, w_in, qw, kw, ind, rc, rsa, rsb, pwbd, pscale, convw, ppool, pconv)


def _attn_kernel(qT_ref, kx_ref, vx_ref, km_ref, o_ref, *, n_blocks):
    TQ = MOBA_BLOCK
    heads_per_pair = PAIR // HEAD_DIM
    km = km_ref[0]
    blk = lax.broadcasted_iota(jnp.int32, (n_blocks, TQ), 0)
    kpos = lax.broadcasted_iota(jnp.int32, (TQ, TQ), 0)
    qpos = lax.broadcasted_iota(jnp.int32, (TQ, TQ), 1)
    causal = kpos <= qpos
    half = lax.broadcasted_iota(jnp.int32, (PAIR, TQ), 0) // HEAD_DIM
    bias_pad = jnp.zeros((BF16_SUBLANES - n_blocks, TQ), F32)
    rhs_pad = jnp.zeros((PAIR - BF16_SUBLANES, TQ), BF16)

    def make_rhs(qi, h):
        p, hh = divmod(h, heads_per_pair)
        lanes = slice(p * PAIR, (p + 1) * PAIR)
        qT_pair = qT_ref[0, lanes, qi * TQ:(qi + 1) * TQ]
        qT_h = jnp.where(half == hh, qT_pair, jnp.zeros_like(qT_pair))
        if qi <= MOBA_TOPK:
            sel = blk <= qi
        else:
            past = blk < qi
            km_pair = km[:, lanes]
            km_hi = km_pair.astype(BF16)
            km_lo = (km_pair - km_hi.astype(F32)).astype(BF16)
            gate = _dot(km_hi, qT_h) + _dot(km_lo, qT_h)
            gate = jnp.where(past, gate, NEG)
            beaten = jnp.zeros((n_blocks, TQ), jnp.int32)
            for m in range(n_blocks):
                gm = gate[m:m + 1, :]
                ahead = (gm > gate) | ((gm == gate) & (m < blk))
                beaten = beaten + ahead.astype(jnp.int32)
            sel = (past & (beaten < MOBA_TOPK)) | (blk == qi)
        bias = jnp.concatenate([jnp.where(sel, 0.0, NEG), bias_pad], axis=0).astype(BF16)
        return jnp.concatenate([qT_h, bias, rhs_pad], axis=0)

    items = [(qi, n, h) for qi in range(n_blocks) for n in [qi] + list(range(qi)) for h in range(ATTN_HEADS)]
    rhs = {}
    run_max = {}
    run_acc = {}
    done = {}

    def scores(item):
        qi, n, h = item
        if (qi, h) not in rhs:
            rhs[(qi, h)] = make_rhs(qi, h)
        p = h // heads_per_pair
        s = _dot(kx_ref[0, n * TQ:(n + 1) * TQ, 2 * p * PAIR:(2 * p + 2) * PAIR], rhs[(qi, h)])
        return s.astype(BF16)

    def values(qi, n, h, pr, alpha):
        acc = _dot(vx_ref[0, h * V_ROWS:(h + 1) * V_ROWS, n * TQ:(n + 1) * TQ], pr)
        if alpha is not None:
            acc = alpha * run_acc[(qi, h)] + acc
        run_acc[(qi, h)] = acc
        last_block = qi - 1 if qi > 0 else 0
        if n == last_block:
            done[h] = acc[0:HEAD_DIM, :] / acc[HEAD_DIM:HEAD_DIM + 1, :]
            del run_acc[(qi, h)], run_max[(qi, h)], rhs[(qi, h)]
            if h == ATTN_HEADS - 1:
                oT = jnp.concatenate([done[j] for j in range(ATTN_HEADS)], axis=0)
                o_ref[0, qi * TQ:(qi + 1) * TQ, :] = oT.T.astype(BF16)

    pending = [scores(it) for it in items[:QK_AHEAD]]
    late = []
    for i, (qi, n, h) in enumerate(items):
        if i + QK_AHEAD < len(items):
            pending.append(scores(items[i + QK_AHEAD]))
        s = pending.pop(0)
        if n == qi:
            s = jnp.where(causal, s, jnp.asarray(NEG, BF16))
        m_n = jnp.max(s, axis=0, keepdims=True).astype(F32)
        alpha = None
        if (qi, h) in run_max:
            m_n = jnp.maximum(run_max[(qi, h)], m_n)
            alpha = jnp.exp2(run_max[(qi, h)] - m_n)
        run_max[(qi, h)] = m_n
        late.append((qi, n, h, jnp.exp2(s - m_n.astype(BF16)), alpha))
        if len(late) > PV_BEHIND:
            values(*late.pop(0))
    for job in late:
        values(*job)


def _attn_call(qT, kx, vx, km):
    B, A, S = qT.shape
    nb = S // MOBA_BLOCK
    assert nb <= BF16_SUBLANES and nb <= PAIR
    kern = functools.partial(_attn_kernel, n_blocks=nb)
    whole = lambda b: (b, 0, 0)
    return pl.pallas_call(
        kern,
        grid=(B,),
        in_specs=[pl.BlockSpec((1, A, S), whole),
                  pl.BlockSpec((1, S, 2 * A), whole),
                  pl.BlockSpec((1, ATTN_HEADS * V_ROWS, S), whole),
                  pl.BlockSpec((1, nb, A), whole)],
        out_specs=pl.BlockSpec((1, S, A), whole),
        out_shape=jax.ShapeDtypeStruct((B, S, A), BF16),
        compiler_params=_params(1),
        name="moba_attention",
    )(qT, kx, vx, km)


def _post_kernel(x_ref, part_ref, ga_ref, ya_ref, mod_ref, n2w_ref,
                 pattn_ref, wout_ref, wg_ref, wu_ref, wd_ref, o_ref, act_ref):
    g1 = mod_ref[2:3, :]
    sh2 = mod_ref[3:4, :]
    sc2 = mod_ref[4:5, :]
    g2 = mod_ref[5:6, :]
    merged = part_ref[0].astype(F32) + ga_ref[0].astype(F32) * _dot(ya_ref[0], pattn_ref[...])
    x1 = x_ref[0] + g1 * _dot(merged.astype(BF16), wout_ref[...])
    ms = jnp.mean(x1 * x1, axis=-1, keepdims=True)
    h2 = (x1 * lax.rsqrt(ms + EPS)) * (n2w_ref[...] * (1.0 + sc2)) + sh2
    h2 = h2.astype(BF16)
    d_ff = wg_ref.shape[1]

    def ff_dots(j):
        cols = slice(j * FF_TN, (j + 1) * FF_TN)
        return _dot(h2, wg_ref[:, cols]), _dot(h2, wu_ref[:, cols])

    res = ff_dots(0)
    for j in range(d_ff // FF_TN):
        nxt = ff_dots(j + 1) if (j + 1) * FF_TN < d_ff else None
        g, u = res
        act_ref[:, j * FF_TN:(j + 1) * FF_TN] = (g * jax.nn.sigmoid(g) * u).astype(BF16)
        res = nxt
    o_ref[0] = x1 + g2 * _dot(act_ref[...], wd_ref[...])


def _post_call(layer, x, part, ga, ya, mod, n2w, pattn, wout, wg, wu, wd):
    B, S, D = x.shape
    A = ya.shape[-1]
    d_ff = wg.shape[-1]
    assert d_ff % FF_TN == 0
    row = lambda b, s: (b, s, 0)
    lspec = functools.partial(_layer_spec, layer)
    return pl.pallas_call(
        _post_kernel,
        grid=(B, S // TM),
        in_specs=[pl.BlockSpec((1, TM, D), row), pl.BlockSpec((1, TM, D), row),
                  pl.BlockSpec((1, TM, D), row), pl.BlockSpec((1, TM, A), row),
                  pl.BlockSpec((None, None) + mod.shape[2:], lambda b, s: (layer, b, 0, 0)),
                  lspec(n2w.shape),
                  lspec(pattn.shape), lspec(wout.shape), lspec(wg.shape), lspec(wu.shape), lspec(wd.shape)],
        out_specs=pl.BlockSpec((1, TM, D), row),
        out_shape=jax.ShapeDtypeStruct((B, S, D), F32),
        scratch_shapes=[pltpu.VMEM((TM, d_ff), BF16)],
        compiler_params=_params(2),
        name="merge_swiglu",
    )(x, part, ga, ya, mod, n2w, pattn, wout, wg, wu, wd)


def kernel(x, c, positions, norm1_w, norm2_w, w_ada, b_ada, w_in, pool_w, pool_scale, conv_w,
           q_norm_w, k_norm_w, p_pool, p_conv, p_attn, w_out, w_gate, w_up, w_down):
    B, S, D = x.shape
    L = w_ada.shape[0]
    assert S % TM == 0 and TM % MOBA_BLOCK == 0 and HALO >= max(POOL_WINDOWS) - 1
    n_groups, gdim = pool_w.shape[1], pool_w.shape[2]
    assert n_groups == len(POOL_WINDOWS) and conv_w.shape[1] == CONV_K

    mod = _modulation(c, w_ada, b_ada).reshape(L, B, 6, D)
    rc, rsa, rsb = _rope_tables(positions)

    head = np.arange(ATTN_WIDTH) // HEAD_DIM
    ind = jnp.asarray((head[:, None] == head[None, :]) / HEAD_DIM, BF16)

    pwbd = jnp.zeros((L, n_groups * gdim, n_groups * gdim), F32)
    for g in range(n_groups):
        pwbd = pwbd.at[:, g * gdim:(g + 1) * gdim, g * gdim:(g + 1) * gdim].set(pool_w[:, g])
    bf = lambda w: w.astype(BF16)
    qw = jnp.tile(q_norm_w, (1, ATTN_HEADS)).reshape(L, 1, ATTN_WIDTH)
    kw = jnp.tile(k_norm_w, (1, ATTN_HEADS)).reshape(L, 1, ATTN_WIDTH)
    in_params = (norm1_w.reshape(L, 1, D), bf(w_in), qw, kw, ind, rc, rsa, rsb, bf(pwbd),
                 pool_scale.reshape(L, 1, -1), conv_w, bf(p_pool), bf(p_conv))
    post_params = (norm2_w.reshape(L, 1, D), bf(p_attn), bf(w_out), bf(w_gate), bf(w_up), bf(w_down))

    for l in range(L):
        part, ga, qT, kx, vx, km = _in_call(l, x, mod, *in_params)
        ya = _attn_call(qT, kx, vx, km.reshape(B, S // MOBA_BLOCK, ATTN_WIDTH))
        x = _post_call(l, x, part, ga, ya, mod, *post_params)
    return x
```

```python
import functools
import math

import numpy as np
import jax
import jax.numpy as jnp
from jax import lax
from jax.experimental import pallas as pl
from jax.experimental.pallas import tpu as pltpu

F32 = jnp.float32
BF16 = jnp.bfloat16

HEAD_DIM = 64
ATTN_HEADS = 8
ATTN_WIDTH = ATTN_HEADS * HEAD_DIM
ROPE_DIM = HEAD_DIM // 4
ROPE_HALF = ROPE_DIM // 2
ROPE_THETA = 500000.0
MOBA_BLOCK = 256
MOBA_TOPK = 3
POOL_WINDOWS = (2, 4, 8, 16)
CONV_K = 3
EPS = 1e-6
NEG = -1e30

LANES = 128
PAIR = LANES
HALO = 16
VMEM_LIMIT = 56 * 1024 * 1024

TM = 512
ADA_TN = 1536
FF_TN = 256
GATE_TN = 512
PV_BEHIND = 1
QK_AHEAD = 5
BF16_SUBLANES = 16
V_ROWS = HEAD_DIM + BF16_SUBLANES
Q_SCALE = (1.0 / math.sqrt(HEAD_DIM)) * math.log2(math.e)


def _dot(a, b):
    return jnp.dot(a, b, preferred_element_type=F32)


def _const_spec(shape):
    nd = len(shape)
    return pl.BlockSpec(shape, lambda *_: (0,) * nd, pipeline_mode=pl.Buffered(1))


def _layer_spec(layer, shape):
    if len(shape) == 2:
        return _const_spec(shape)
    nd = len(shape)
    return pl.BlockSpec((None,) + tuple(shape[1:]), lambda *_: (layer,) + (0,) * (nd - 1),
                        pipeline_mode=pl.Buffered(1))


def _params(n_axes):
    return pltpu.CompilerParams(dimension_semantics=("arbitrary",) * n_axes,
                                vmem_limit_bytes=VMEM_LIMIT)


def _mod_kernel(c_ref, w_ref, b_ref, o_ref):
    c = c_ref[...]
    c_act = (c * jax.nn.sigmoid(c)).astype(BF16)
    o_ref[0] = _dot(c_act, w_ref[0].astype(BF16)) + b_ref[0]


def _modulation(c, w_ada, b_ada):
    L, D, N = w_ada.shape
    B = c.shape[0]
    return pl.pallas_call(
        _mod_kernel,
        grid=(L, N // ADA_TN),
        in_specs=[pl.BlockSpec((B, D), lambda l, j: (0, 0)),
                  pl.BlockSpec((1, D, ADA_TN), lambda l, j: (l, 0, j)),
                  pl.BlockSpec((1, 1, ADA_TN), lambda l, j: (l, 0, j))],
        out_specs=pl.BlockSpec((1, B, ADA_TN), lambda l, j: (l, 0, j)),
        out_shape=jax.ShapeDtypeStruct((L, B, N), F32),
        compiler_params=_params(2),
        name="adaln_modulation",
    )(c, w_ada, b_ada.reshape(L, 1, N))


def _rope_kernel(pos_ref, freq_ref, c_ref, sa_ref, sb_ref):
    ang = pos_ref[0].astype(F32) * freq_ref[...]
    cs = jnp.cos(ang)
    sn = jnp.sin(ang)
    d = lax.broadcasted_iota(jnp.int32, (1, LANES), 1) & (HEAD_DIM - 1)
    c_ref[0] = jnp.where(d < ROPE_DIM, cs, 1.0)
    sa_ref[0] = jnp.where(d < ROPE_HALF, -sn, 0.0)
    sb_ref[0] = jnp.where((d >= ROPE_HALF) & (d < ROPE_DIM), sn, 0.0)


def _rope_tables(positions):
    B, S = positions.shape
    d = np.arange(LANES) % HEAD_DIM
    freq = np.where(d < ROPE_DIM, ROPE_THETA ** (-(2.0 * (d % ROPE_HALF)) / ROPE_DIM), 0.0)
    freq = jnp.asarray(freq.reshape(1, LANES), F32)
    spec = pl.BlockSpec((1, S, LANES), lambda b: (b, 0, 0))
    shp = jax.ShapeDtypeStruct((B, S, LANES), F32)
    return pl.pallas_call(
        _rope_kernel,
        grid=(B,),
        in_specs=[pl.BlockSpec((1, S, 1), lambda b: (b, 0, 0)),
                  pl.BlockSpec((1, LANES), lambda b: (0, 0))],
        out_specs=(spec, spec, spec),
        out_shape=(shp, shp, shp),
        compiler_params=_params(1),
        name="rotary_tables",
    )(positions.reshape(B, S, 1), freq)


def _in_kernel(x_ref, mod_ref, n1w_ref, win_ref, qw_ref, kw_ref, ind_ref,
               rc_ref, rsa_ref, rsb_ref, pwbd_ref, pscale_ref, convw_ref, ppool_ref, pconv_ref,
               part_ref, ga_ref, qT_ref, kx_ref, vx_ref, km_ref, ext_ref,
               *, d_model, pool_w, conv_w, attn_w):
    si = pl.program_id(1)
    x = x_ref[0]
    ms = jnp.mean(x * x, axis=-1, keepdims=True)
    sh1 = mod_ref[0:1, :]
    sc1 = mod_ref[1:2, :]
    h = (x * lax.rsqrt(ms + EPS)) * (n1w_ref[...] * (1.0 + sc1)) + sh1
    h = h.astype(BF16)

    @pl.when(si == 0)
    def _():
        ext_ref[0:HALO, :] = jnp.zeros((HALO, pool_w + conv_w), F32)

    o_pool = 0
    o_uc = o_pool + pool_w
    o_bc = o_uc + conv_w
    o_cc = o_bc + conv_w
    o_q = o_cc + conv_w
    o_k = o_q + attn_w
    o_v = o_k + attn_w
    o_g = o_v + attn_w
    n_pairs = attn_w // PAIR

    rc = rc_ref[0]
    rsa = rsa_ref[0]
    rsb = rsb_ref[0]

    def norm_rope(u, w_tiled):
        ss = _dot((u * u).astype(BF16), ind_ref[...])
        un = u * lax.rsqrt(ss + EPS) * w_tiled
        outs = []
        for p in range(n_pairs):
            cp = un[:, p * PAIR:(p + 1) * PAIR]
            outs.append(cp * rc + pltpu.roll(cp, PAIR - ROPE_HALF, 1) * rsa + pltpu.roll(cp, ROPE_HALF, 1) * rsb)
        return outs

    def proj(lo, hi):
        return lambda: (_dot(h, win_ref[:, lo:hi]),)

    def q_stage(uq):
        for p, qp in enumerate(norm_rope(uq, qw_ref[...])):
            qT_ref[0, p * PAIR:(p + 1) * PAIR, :] = (qp * Q_SCALE).T.astype(BF16)

    def k_stage(uk):
        row_blk = si * (TM // MOBA_BLOCK) + lax.broadcasted_iota(jnp.int32, (TM, PAIR), 0) // MOBA_BLOCK
        onehot = jnp.where(lax.broadcasted_iota(jnp.int32, (TM, PAIR), 1) == row_blk, 1.0, 0.0).astype(BF16)
        for p, kp in enumerate(norm_rope(uk, kw_ref[...])):
            kx_ref[0, :, 2 * p * PAIR:(2 * p + 1) * PAIR] = kp.astype(BF16)
            kx_ref[0, :, (2 * p + 1) * PAIR:(2 * p + 2) * PAIR] = onehot
            for j in range(TM // MOBA_BLOCK):
                km_ref[0, 0, j:j + 1, p * PAIR:(p + 1) * PAIR] = jnp.mean(
                    kp[j * MOBA_BLOCK:(j + 1) * MOBA_BLOCK, :], axis=0, keepdims=True)

    def v_stage(uv):
        ones_rows = jnp.where(lax.broadcasted_iota(jnp.int32, (V_ROWS - HEAD_DIM, TM), 0) == 0, 1.0, 0.0).astype(BF16)
        for p in range(n_pairs):
            vT = uv[:, p * PAIR:(p + 1) * PAIR].T.astype(BF16)
            for hh in range(PAIR // HEAD_DIM):
                r0 = (p * (PAIR // HEAD_DIM) + hh) * V_ROWS
                vx_ref[0, r0:r0 + HEAD_DIM, :] = vT[hh * HEAD_DIM:(hh + 1) * HEAD_DIM, :]
                vx_ref[0, r0 + HEAD_DIM:r0 + V_ROWS, :] = ones_rows

    mixed = {}

    def mixer_stage(ua):
        up = ua[:, o_pool:o_uc]
        uc = ua[:, o_uc:o_bc]
        bc = ua[:, o_bc:o_cc]
        cc = ua[:, o_cc:o_q]
        ext_ref[HALO:HALO + TM, 0:pool_w] = up
        ext_ref[HALO:HALO + TM, pool_w:pool_w + conv_w] = cc * uc
        ext = ext_ref[...]
        ext_ref[0:HALO, :] = ext[TM:TM + HALO, :]
        e_p = ext[:, 0:pool_w]
        e_c = ext[:, pool_w:pool_w + conv_w]

        s2 = e_p + pltpu.roll(e_p, 1, 0)
        s4 = s2 + pltpu.roll(s2, 2, 0)
        s8 = s4 + pltpu.roll(s4, 4, 0)
        s16 = s8 + pltpu.roll(s8, 8, 0)
        gdim = pool_w // len(POOL_WINDOWS)
        lane = lax.broadcasted_iota(jnp.int32, (1, pool_w), 1)
        wsum = jnp.where(lane < gdim, s2, jnp.where(lane < 2 * gdim, s4, jnp.where(lane < 3 * gdim, s8, s16)))
        win = jnp.where(lane < gdim, POOL_WINDOWS[0],
                        jnp.where(lane < 2 * gdim, POOL_WINDOWS[1],
                                  jnp.where(lane < 3 * gdim, POOL_WINDOWS[2], POOL_WINDOWS[3])))
        t = si * TM + lax.broadcasted_iota(jnp.int32, (TM, pool_w), 0)
        cnt = jnp.minimum(t + 1, win).astype(F32)
        pooled = wsum[HALO:, :] / cnt - up
        yp = _dot(pooled.astype(BF16), pwbd_ref[...]) * pscale_ref[...]
        mixed["pool"] = _dot(yp.astype(BF16), ppool_ref[...])

        cw = convw_ref[...]
        conv = cw[0:1, :] * pltpu.roll(e_c, 2, 0) + cw[1:2, :] * pltpu.roll(e_c, 1, 0) + cw[2:3, :] * e_c
        yc = bc * conv[HALO:, :]
        mixed["conv"] = _dot(yc.astype(BF16), pconv_ref[...])

    def gate_dots(c):
        cols = [slice(o_g + i * d_model + c * GATE_TN, o_g + i * d_model + (c + 1) * GATE_TN) for i in range(3)]
        return lambda: tuple(_dot(h, win_ref[:, cs]) for cs in cols)

    def gate_stage(c):
        out = slice(c * GATE_TN, (c + 1) * GATE_TN)

        def run(g0, g1, g2):
            merged = jax.nn.sigmoid(g0) * mixed["pool"][:, out] + jax.nn.sigmoid(g1) * mixed["conv"][:, out]
            part_ref[0, :, out] = merged.astype(BF16)
            ga_ref[0, :, out] = jax.nn.sigmoid(g2).astype(BF16)
        return run

    stages = [(proj(o_q, o_k), q_stage), (proj(o_k, o_v), k_stage), (proj(o_v, o_g), v_stage),
              (proj(o_pool, o_q), mixer_stage)]
    stages += [(gate_dots(c), gate_stage(c)) for c in range(d_model // GATE_TN)]
    res = stages[0][0]()
    for i, (_, finish) in enumerate(stages):
        nxt = stages[i + 1][0]() if i + 1 < len(stages) else None
        finish(*res)
        res = nxt


def _in_call(layer, x, mod, n1w, w_in, qw, kw, ind, rc, rsa, rsb, pwbd, pscale, convw, ppool, pconv):
    B, S, D = x.shape
    pool_w = pwbd.shape[-1]
    conv_w = convw.shape[-1]
    A = ATTN_WIDTH
    nt = S // TM
    row = lambda b, s: (b, s, 0)
    tr = lambda b, s: (b, 0, s)
    lspec = functools.partial(_layer_spec, layer)
    kern = functools.partial(_in_kernel, d_model=D, pool_w=pool_w, conv_w=conv_w, attn_w=A)
    return pl.pallas_call(
        kern,
        grid=(B, nt),
        in_specs=[pl.BlockSpec((1, TM, D), row),
                  pl.BlockSpec((None, None) + mod.shape[2:], lambda b, s: (layer, b, 0, 0)),
                  lspec(n1w.shape), lspec(w_in.shape),
                  lspec(qw.shape), lspec(kw.shape), _const_spec((A, A)),
                  pl.BlockSpec((1, TM, LANES), row), pl.BlockSpec((1, TM, LANES), row),
                  pl.BlockSpec((1, TM, LANES), row),
                  lspec(pwbd.shape), lspec(pscale.shape), lspec(convw.shape),
                  lspec(ppool.shape), lspec(pconv.shape)],
        out_specs=(pl.BlockSpec((1, TM, D), row), pl.BlockSpec((1, TM, D), row),
                   pl.BlockSpec((1, A, TM), tr), pl.BlockSpec((1, TM, 2 * A), row),
                   pl.BlockSpec((1, ATTN_HEADS * V_ROWS, TM), tr),
                   pl.BlockSpec((1, 1, TM // MOBA_BLOCK, A), lambda b, s: (b, s, 0, 0))),
        out_shape=(jax.ShapeDtypeStruct((B, S, D), BF16), jax.ShapeDtypeStruct((B, S, D), BF16),
                   jax.ShapeDtypeStruct((B, A, S), BF16), jax.ShapeDtypeStruct((B, S, 2 * A), BF16),
                   jax.ShapeDtypeStruct((B, ATTN_HEADS * V_ROWS, S), BF16),
                   jax.ShapeDtypeStruct((B, nt, TM // MOBA_BLOCK, A), F32)),
        scratch_shapes=[pltpu.VMEM((TM + HALO, pool_w + conv_w), F32)],
        compiler_params=_params(2),
        name="in_proj_mixers",
    )(x, mod, n1w, w_in, qw, kw, ind, rc, rsa, rsb, pwbd, pscale, convw, ppool, pconv)


def _attn_kernel(qT_ref, kx_ref, vx_ref, km_ref, *refs, n_blocks, n_cast):
    o_ref = refs[n_cast]
    for w_ref, wb_ref in zip(refs[:n_cast], refs[n_cast + 1:]):
        wb_ref[...] = w_ref[...].astype(BF16)

    TQ = MOBA_BLOCK
    heads_per_pair = PAIR // HEAD_DIM
    km = km_ref[0]
    blk = lax.broadcasted_iota(jnp.int32, (n_blocks, TQ), 0)
    kpos = lax.broadcasted_iota(jnp.int32, (TQ, TQ), 0)
    qpos = lax.broadcasted_iota(jnp.int32, (TQ, TQ), 1)
    causal = kpos <= qpos
    half = lax.broadcasted_iota(jnp.int32, (PAIR, TQ), 0) // HEAD_DIM
    bias_pad = jnp.zeros((BF16_SUBLANES - n_blocks, TQ), F32)
    rhs_pad = jnp.zeros((PAIR - BF16_SUBLANES, TQ), BF16)

    def make_rhs(qi, h):
        p, hh = divmod(h, heads_per_pair)
        lanes = slice(p * PAIR, (p + 1) * PAIR)
        qT_pair = qT_ref[0, lanes, qi * TQ:(qi + 1) * TQ]
        qT_h = jnp.where(half == hh, qT_pair, jnp.zeros_like(qT_pair))
        if qi <= MOBA_TOPK:
            sel = blk <= qi
        else:
            past = blk < qi
            km_pair = km[:, lanes]
            km_hi = km_pair.astype(BF16)
            km_lo = (km_pair - km_hi.astype(F32)).astype(BF16)
            gate = _dot(km_hi, qT_h) + _dot(km_lo, qT_h)
            gate = jnp.where(past, gate, NEG)
            beaten = jnp.zeros((n_blocks, TQ), jnp.int32)
            for m in range(n_blocks):
                gm = gate[m:m + 1, :]
                ahead = (gm > gate) | ((gm == gate) & (m < blk))
                beaten = beaten + ahead.astype(jnp.int32)
            sel = (past & (beaten < MOBA_TOPK)) | (blk == qi)
        bias = jnp.concatenate([jnp.where(sel, 0.0, NEG), bias_pad], axis=0).astype(BF16)
        return jnp.concatenate([qT_h, bias, rhs_pad], axis=0)

    items = [(qi, n, h) for qi in range(n_blocks) for n in [qi] + list(range(qi)) for h in range(ATTN_HEADS)]
    rhs = {}
    run_max = {}
    run_acc = {}
    done = {}

    def scores(item):
        qi, n, h = item
        if (qi, h) not in rhs:
            rhs[(qi, h)] = make_rhs(qi, h)
        p = h // heads_per_pair
        s = _dot(kx_ref[0, n * TQ:(n + 1) * TQ, 2 * p * PAIR:(2 * p + 2) * PAIR], rhs[(qi, h)])
        return s.astype(BF16)

    def values(qi, n, h, pr, alpha):
        acc = _dot(vx_ref[0, h * V_ROWS:(h + 1) * V_ROWS, n * TQ:(n + 1) * TQ], pr)
        if alpha is not None:
            acc = alpha * run_acc[(qi, h)] + acc
        run_acc[(qi, h)] = acc
        last_block = qi - 1 if qi > 0 else 0
        if n == last_block:
            done[h] = acc[0:HEAD_DIM, :] / acc[HEAD_DIM:HEAD_DIM + 1, :]
            del run_acc[(qi, h)], run_max[(qi, h)], rhs[(qi, h)]
            if h == ATTN_HEADS - 1:
                oT = jnp.concatenate([done[j] for j in range(ATTN_HEADS)], axis=0)
                o_ref[0, qi * TQ:(qi + 1) * TQ, :] = oT.T.astype(BF16)

    pending = [scores(it) for it in items[:QK_AHEAD]]
    late = []
    for i, (qi, n, h) in enumerate(items):
        if i + QK_AHEAD < len(items):
            pending.append(scores(items[i + QK_AHEAD]))
        s = pending.pop(0)
        if n == qi:
            s = jnp.where(causal, s, jnp.asarray(NEG, BF16))
        m_n = jnp.max(s, axis=0, keepdims=True).astype(F32)
        alpha = None
        if (qi, h) in run_max:
            m_n = jnp.maximum(run_max[(qi, h)], m_n)
            alpha = jnp.exp2(run_max[(qi, h)] - m_n)
        run_max[(qi, h)] = m_n
        late.append((qi, n, h, jnp.exp2(s - m_n.astype(BF16)), alpha))
        if len(late) > PV_BEHIND:
            values(*late.pop(0))
    for job in late:
        values(*job)


def _attn_call(qT, kx, vx, km, casts):
    B, A, S = qT.shape
    nb = S // MOBA_BLOCK
    assert nb <= BF16_SUBLANES and nb <= PAIR
    kern = functools.partial(_attn_kernel, n_blocks=nb, n_cast=len(casts))
    whole = lambda b: (b, 0, 0)
    cast_in, cast_out, cast_shape = [], [], []
    for w, layer in casts:
        _, rows, cols = w.shape
        assert rows % (B * BF16_SUBLANES) == 0
        cast_in.append(pl.BlockSpec((None, rows // B, cols), lambda b, layer=layer: (layer, b, 0)))
        cast_out.append(pl.BlockSpec((rows // B, cols), lambda b: (b, 0)))
        cast_shape.append(jax.ShapeDtypeStruct((rows, cols), BF16))
    out = pl.pallas_call(
        kern,
        grid=(B,),
        in_specs=[pl.BlockSpec((1, A, S), whole),
                  pl.BlockSpec((1, S, 2 * A), whole),
                  pl.BlockSpec((1, ATTN_HEADS * V_ROWS, S), whole),
                  pl.BlockSpec((1, nb, A), whole)] + cast_in,
        out_specs=[pl.BlockSpec((1, S, A), whole)] + cast_out,
        out_shape=[jax.ShapeDtypeStruct((B, S, A), BF16)] + cast_shape,
        compiler_params=_params(1),
        name="moba_attention",
    )(qT, kx, vx, km, *[w for w, _ in casts])
    return out[0], out[1:]


def _post_kernel(x_ref, part_ref, ga_ref, ya_ref, mod_ref, n2w_ref,
                 pattn_ref, wout_ref, wg_ref, wu_ref, wd_ref, o_ref, act_ref):
    g1 = mod_ref[2:3, :]
    sh2 = mod_ref[3:4, :]
    sc2 = mod_ref[4:5, :]
    g2 = mod_ref[5:6, :]
    merged = part_ref[0].astype(F32) + ga_ref[0].astype(F32) * _dot(ya_ref[0], pattn_ref[...])
    x1 = x_ref[0] + g1 * _dot(merged.astype(BF16), wout_ref[...])
    ms = jnp.mean(x1 * x1, axis=-1, keepdims=True)
    h2 = (x1 * lax.rsqrt(ms + EPS)) * (n2w_ref[...] * (1.0 + sc2)) + sh2
    h2 = h2.astype(BF16)
    d_ff = wg_ref.shape[1]

    def ff_dots(j):
        cols = slice(j * FF_TN, (j + 1) * FF_TN)
        return _dot(h2, wg_ref[:, cols]), _dot(h2, wu_ref[:, cols])

    res = ff_dots(0)
    for j in range(d_ff // FF_TN):
        nxt = ff_dots(j + 1) if (j + 1) * FF_TN < d_ff else None
        g, u = res
        act_ref[:, j * FF_TN:(j + 1) * FF_TN] = (g * jax.nn.sigmoid(g) * u).astype(BF16)
        res = nxt
    o_ref[0] = x1 + g2 * _dot(act_ref[...], wd_ref[...])


def _post_call(layer, x, part, ga, ya, mod, n2w, pattn, wout, wg, wu, wd):
    B, S, D = x.shape
    A = ya.shape[-1]
    d_ff = wg.shape[-1]
    assert d_ff % FF_TN == 0
    row = lambda b, s: (b, s, 0)
    lspec = functools.partial(_layer_spec, layer)
    return pl.pallas_call(
        _post_kernel,
        grid=(B, S // TM),
        in_specs=[pl.BlockSpec((1, TM, D), row), pl.BlockSpec((1, TM, D), row),
                  pl.BlockSpec((1, TM, D), row), pl.BlockSpec((1, TM, A), row),
                  pl.BlockSpec((None, None) + mod.shape[2:], lambda b, s: (layer, b, 0, 0)),
                  lspec(n2w.shape),
                  lspec(pattn.shape), lspec(wout.shape), lspec(wg.shape), lspec(wu.shape), lspec(wd.shape)],
        out_specs=pl.BlockSpec((1, TM, D), row),
        out_shape=jax.ShapeDtypeStruct((B, S, D), F32),
        scratch_shapes=[pltpu.VMEM((TM, d_ff), BF16)],
        compiler_params=_params(2),
        name="merge_swiglu",
    )(x, part, ga, ya, mod, n2w, pattn, wout, wg, wu, wd)


def kernel(x, c, positions, norm1_w, norm2_w, w_ada, b_ada, w_in, pool_w, pool_scale, conv_w,
           q_norm_w, k_norm_w, p_pool, p_conv, p_attn, w_out, w_gate, w_up, w_down):
    B, S, D = x.shape
    L = w_ada.shape[0]
    assert S % TM == 0 and TM % MOBA_BLOCK == 0 and HALO >= max(POOL_WINDOWS) - 1
    n_groups, gdim = pool_w.shape[1], pool_w.shape[2]
    assert n_groups == len(POOL_WINDOWS) and conv_w.shape[1] == CONV_K

    mod = _modulation(c, w_ada, b_ada).reshape(L, B, 6, D)
    rc, rsa, rsb = _rope_tables(positions)

    head = np.arange(ATTN_WIDTH) // HEAD_DIM
    ind = jnp.asarray((head[:, None] == head[None, :]) / HEAD_DIM, BF16)

    pwbd = jnp.zeros((L, n_groups * gdim, n_groups * gdim), F32)
    for g in range(n_groups):
        pwbd = pwbd.at[:, g * gdim:(g + 1) * gdim, g * gdim:(g + 1) * gdim].set(pool_w[:, g])
    pwbd = pwbd.astype(BF16)
    qw = jnp.tile(q_norm_w, (1, ATTN_HEADS)).reshape(L, 1, ATTN_WIDTH)
    kw = jnp.tile(k_norm_w, (1, ATTN_HEADS)).reshape(L, 1, ATTN_WIDTH)
    n1w = norm1_w.reshape(L, 1, D)
    n2w = norm2_w.reshape(L, 1, D)
    pscale = pool_scale.reshape(L, 1, -1)

    in_f32 = (w_in, p_pool, p_conv)
    post_f32 = (p_attn, w_out, w_gate, w_up, w_down)
    in_bf = [w[0].astype(BF16) for w in in_f32]

    for l in range(L):
        part, ga, qT, kx, vx, km = _in_call(l, x, mod, n1w, in_bf[0], qw, kw, ind, rc, rsa, rsb, pwbd,
                                            pscale, conv_w, in_bf[1], in_bf[2])
        casts = [(w, l) for w in post_f32] + ([(w, l + 1) for w in in_f32] if l + 1 < L else [])
        ya, cast = _attn_call(qT, kx, vx, km.reshape(B, S // MOBA_BLOCK, ATTN_WIDTH), casts)
        x = _post_call(l, x, part, ga, ya, mod, n2w, *cast[:len(post_f32)])
        in_bf = cast[len(post_f32):]
    return x
```

```python
import functools
import math

import numpy as np
import jax
import jax.numpy as jnp
from jax import lax
from jax.experimental import pallas as pl
from jax.experimental.pallas import tpu as pltpu

F32 = jnp.float32
BF16 = jnp.bfloat16

HEAD_DIM = 64
ATTN_HEADS = 8
ATTN_WIDTH = ATTN_HEADS * HEAD_DIM
ROPE_DIM = HEAD_DIM // 4
ROPE_HALF = ROPE_DIM // 2
ROPE_THETA = 500000.0
MOBA_BLOCK = 256
MOBA_TOPK = 3
POOL_WINDOWS = (2, 4, 8, 16)
CONV_K = 3
EPS = 1e-6
NEG = -1e30

LANES = 128
PAIR = LANES
HALO = 16
VMEM_LIMIT = 56 * 1024 * 1024

TM = 512
SUB = TM
ADA_TN = 1536
FF_TN = 256
GATE_TN = 512
PV_BEHIND = 1
QK_AHEAD = 5
BF16_SUBLANES = 16
V_ROWS = HEAD_DIM + BF16_SUBLANES
Q_SCALE = (1.0 / math.sqrt(HEAD_DIM)) * math.log2(math.e)


def _dot(a, b):
    return jnp.dot(a, b, preferred_element_type=F32)


def _const_spec(shape):
    nd = len(shape)
    return pl.BlockSpec(shape, lambda *_: (0,) * nd, pipeline_mode=pl.Buffered(1))


def _layer_spec(layer, shape):
    if len(shape) == 2:
        return _const_spec(shape)
    nd = len(shape)
    return pl.BlockSpec((None,) + tuple(shape[1:]), lambda *_: (layer,) + (0,) * (nd - 1),
                        pipeline_mode=pl.Buffered(1))


def _params(n_axes):
    return pltpu.CompilerParams(dimension_semantics=("arbitrary",) * n_axes,
                                vmem_limit_bytes=VMEM_LIMIT)


def _mod_kernel(c_ref, w_ref, b_ref, o_ref):
    c = c_ref[...]
    c_act = (c * jax.nn.sigmoid(c)).astype(BF16)
    o_ref[0] = _dot(c_act, w_ref[0].astype(BF16)) + b_ref[0]


def _modulation(c, w_ada, b_ada):
    L, D, N = w_ada.shape
    B = c.shape[0]
    return pl.pallas_call(
        _mod_kernel,
        grid=(L, N // ADA_TN),
        in_specs=[pl.BlockSpec((B, D), lambda l, j: (0, 0)),
                  pl.BlockSpec((1, D, ADA_TN), lambda l, j: (l, 0, j)),
                  pl.BlockSpec((1, 1, ADA_TN), lambda l, j: (l, 0, j))],
        out_specs=pl.BlockSpec((1, B, ADA_TN), lambda l, j: (l, 0, j)),
        out_shape=jax.ShapeDtypeStruct((L, B, N), F32),
        compiler_params=_params(2),
        name="adaln_modulation",
    )(c, w_ada, b_ada.reshape(L, 1, N))


def _rope_kernel(pos_ref, freq_ref, c_ref, s_ref):
    ang = freq_ref[...] * pos_ref[0].astype(F32)
    c_ref[0] = jnp.cos(ang)
    s_ref[0] = jnp.sin(ang)


def _rope_tables(positions):
    B, S = positions.shape
    freq = ROPE_THETA ** (-np.arange(0, ROPE_DIM, 2, dtype=np.float64) / ROPE_DIM)
    freq = jnp.asarray(freq.reshape(ROPE_HALF, 1), F32)
    spec = pl.BlockSpec((1, ROPE_HALF, S), lambda b: (b, 0, 0))
    shp = jax.ShapeDtypeStruct((B, ROPE_HALF, S), F32)
    return pl.pallas_call(
        _rope_kernel,
        grid=(B,),
        in_specs=[pl.BlockSpec((1, 1, S), lambda b: (b, 0, 0)),
                  pl.BlockSpec((ROPE_HALF, 1), lambda b: (0, 0))],
        out_specs=(spec, spec),
        out_shape=(shp, shp),
        compiler_params=_params(1),
        name="rotary_tables",
    )(positions.reshape(B, 1, S), freq)


def _in_kernel(x_ref, mod_ref, n1w_ref, win_ref, qw_ref, kw_ref,
               cos_ref, sin_ref, pwbd_ref, pscale_ref, convw_ref, ppool_ref, pconv_ref,
               part_ref, ga_ref, qT_ref, kx_ref, vx_ref, km_ref, ext_ref,
               *, d_model, pool_w, conv_w, attn_w):
    si = pl.program_id(1)
    n_sub = TM // SUB
    hs = {}

    def adaln(sub):
        x = x_ref[0, sub * SUB:(sub + 1) * SUB, :]
        ms = jnp.mean(x * x, axis=-1, keepdims=True)
        sh1 = mod_ref[0:1, :]
        sc1 = mod_ref[1:2, :]
        hs[sub] = ((x * lax.rsqrt(ms + EPS)) * (n1w_ref[...] * (1.0 + sc1)) + sh1).astype(BF16)

    @pl.when(si == 0)
    def _():
        ext_ref[0:HALO, :] = jnp.zeros((HALO, pool_w + conv_w), F32)

    o_pool = 0
    o_uc = o_pool + pool_w
    o_bc = o_uc + conv_w
    o_cc = o_bc + conv_w
    o_q = o_cc + conv_w
    o_k = o_q + attn_w
    o_v = o_k + attn_w
    o_g = o_v + attn_w
    n_pairs = attn_w // PAIR

    def sub_stages(sub):
        rows = slice(sub * SUB, (sub + 1) * SUB)
        blk0 = sub * (SUB // MOBA_BLOCK)
        cos_t = cos_ref[0, :, rows]
        sin_t = sin_ref[0, :, rows]

        def norm_rope_t(u, gain_ref):
            gain = gain_ref[...]
            outs = []
            for p in range(n_pairs):
                u_t = u[:, p * PAIR:(p + 1) * PAIR].T
                parts = []
                for hh in range(PAIR // HEAD_DIM):
                    xh = u_t[hh * HEAD_DIM:(hh + 1) * HEAD_DIM, :]
                    xn = xh * lax.rsqrt(jnp.mean(xh * xh, axis=0, keepdims=True) + EPS) * gain
                    x1 = xn[0:ROPE_HALF, :]
                    x2 = xn[ROPE_HALF:ROPE_DIM, :]
                    parts += [x1 * cos_t - x2 * sin_t, x2 * cos_t + x1 * sin_t, xn[ROPE_DIM:, :]]
                outs.append(jnp.concatenate(parts, axis=0))
            return outs

        def proj(lo, hi):
            return lambda: (_dot(hs[sub], win_ref[:, lo:hi]),)

        def q_stage(uq):
            for p, qp_t in enumerate(norm_rope_t(uq, qw_ref)):
                qT_ref[0, p * PAIR:(p + 1) * PAIR, rows] = (qp_t * Q_SCALE).astype(BF16)

        def k_stage(uk):
            row_blk = (si * (TM // MOBA_BLOCK) + blk0
                       + lax.broadcasted_iota(jnp.int32, (SUB, PAIR), 0) // MOBA_BLOCK)
            onehot = jnp.where(lax.broadcasted_iota(jnp.int32, (SUB, PAIR), 1) == row_blk, 1.0, 0.0).astype(BF16)
            for p, kp_t in enumerate(norm_rope_t(uk, kw_ref)):
                kp = kp_t.T
                kx_ref[0, rows, 2 * p * PAIR:(2 * p + 1) * PAIR] = kp.astype(BF16)
                kx_ref[0, rows, (2 * p + 1) * PAIR:(2 * p + 2) * PAIR] = onehot
                for j in range(SUB // MOBA_BLOCK):
                    km_ref[0, 0, blk0 + j:blk0 + j + 1, p * PAIR:(p + 1) * PAIR] = jnp.mean(
                        kp[j * MOBA_BLOCK:(j + 1) * MOBA_BLOCK, :], axis=0, keepdims=True)

        def v_stage(uv):
            ones_rows = jnp.where(lax.broadcasted_iota(jnp.int32, (V_ROWS - HEAD_DIM, SUB), 0) == 0,
                                  1.0, 0.0).astype(BF16)
            for p in range(n_pairs):
                vT = uv[:, p * PAIR:(p + 1) * PAIR].T.astype(BF16)
                for hh in range(PAIR // HEAD_DIM):
                    r0 = (p * (PAIR // HEAD_DIM) + hh) * V_ROWS
                    vx_ref[0, r0:r0 + HEAD_DIM, rows] = vT[hh * HEAD_DIM:(hh + 1) * HEAD_DIM, :]
                    vx_ref[0, r0 + HEAD_DIM:r0 + V_ROWS, rows] = ones_rows

        mixed = {}

        def mixer_stage(ua):
            up = ua[:, o_pool:o_uc]
            uc = ua[:, o_uc:o_bc]
            bc = ua[:, o_bc:o_cc]
            cc = ua[:, o_cc:o_q]
            ext_ref[HALO:HALO + SUB, 0:pool_w] = up
            ext_ref[HALO:HALO + SUB, pool_w:pool_w + conv_w] = cc * uc
            ext = ext_ref[...]
            ext_ref[0:HALO, :] = ext[SUB:SUB + HALO, :]
            e_p = ext[:, 0:pool_w]
            e_c = ext[:, pool_w:pool_w + conv_w]

            s2 = e_p + pltpu.roll(e_p, 1, 0)
            s4 = s2 + pltpu.roll(s2, 2, 0)
            s8 = s4 + pltpu.roll(s4, 4, 0)
            s16 = s8 + pltpu.roll(s8, 8, 0)
            gdim = pool_w // len(POOL_WINDOWS)
            lane = lax.broadcasted_iota(jnp.int32, (1, pool_w), 1)
            wsum = jnp.where(lane < gdim, s2, jnp.where(lane < 2 * gdim, s4, jnp.where(lane < 3 * gdim, s8, s16)))
            win = jnp.where(lane < gdim, POOL_WINDOWS[0],
                            jnp.where(lane < 2 * gdim, POOL_WINDOWS[1],
                                      jnp.where(lane < 3 * gdim, POOL_WINDOWS[2], POOL_WINDOWS[3])))
            t = si * TM + sub * SUB + lax.broadcasted_iota(jnp.int32, (SUB, pool_w), 0)
            cnt = jnp.minimum(t + 1, win).astype(F32)
            pooled = wsum[HALO:, :] / cnt - up
            yp = _dot(pooled.astype(BF16), pwbd_ref[...]) * pscale_ref[...]
            mixed["pool"] = _dot(yp.astype(BF16), ppool_ref[...])

            cw = convw_ref[...]
            conv = cw[0:1, :] * pltpu.roll(e_c, 2, 0) + cw[1:2, :] * pltpu.roll(e_c, 1, 0) + cw[2:3, :] * e_c
            yc = bc * conv[HALO:, :]
            mixed["conv"] = _dot(yc.astype(BF16), pconv_ref[...])

        def gate_dots(c):
            cols = [slice(o_g + i * d_model + c * GATE_TN, o_g + i * d_model + (c + 1) * GATE_TN) for i in range(3)]
            return lambda: tuple(_dot(hs[sub], win_ref[:, cs]) for cs in cols)

        def gate_stage(c):
            out = slice(c * GATE_TN, (c + 1) * GATE_TN)

            def run(g0, g1, g2):
                merged = jax.nn.sigmoid(g0) * mixed["pool"][:, out] + jax.nn.sigmoid(g1) * mixed["conv"][:, out]
                part_ref[0, rows, out] = merged.astype(BF16)
                ga_ref[0, rows, out] = jax.nn.sigmoid(g2).astype(BF16)
            return run

        stages = [(proj(o_q, o_k), q_stage), (proj(o_k, o_v), k_stage)]
        if sub + 1 < n_sub:
            stages.append((lambda: (), lambda: adaln(sub + 1)))
        stages += [(proj(o_v, o_g), v_stage), (proj(o_pool, o_q), mixer_stage)]
        stages += [(gate_dots(c), gate_stage(c)) for c in range(d_model // GATE_TN)]
        return stages

    adaln(0)
    stages = [st for sub in range(n_sub) for st in sub_stages(sub)]
    res = stages[0][0]()
    for i, (_, finish) in enumerate(stages):
        nxt = stages[i + 1][0]() if i + 1 < len(stages) else None
        finish(*res)
        res = nxt


def _in_call(layer, x, mod, n1w, w_in, qw, kw, cos_t, sin_t, pwbd, pscale, convw, ppool, pconv):
    B, S, D = x.shape
    pool_w = pwbd.shape[-1]
    conv_w = convw.shape[-1]
    A = ATTN_WIDTH
    nt = S // TM
    row = lambda b, s: (b, s, 0)
    tr = lambda b, s: (b, 0, s)
    lspec = functools.partial(_layer_spec, layer)
    kern = functools.partial(_in_kernel, d_model=D, pool_w=pool_w, conv_w=conv_w, attn_w=A)
    return pl.pallas_call(
        kern,
        grid=(B, nt),
        in_specs=[pl.BlockSpec((1, TM, D), row),
                  pl.BlockSpec((None, None) + mod.shape[2:], lambda b, s: (layer, b, 0, 0)),
                  lspec(n1w.shape), lspec(w_in.shape),
                  lspec(qw.shape), lspec(kw.shape),
                  pl.BlockSpec((1, ROPE_HALF, TM), tr), pl.BlockSpec((1, ROPE_HALF, TM), tr),
                  lspec(pwbd.shape), lspec(pscale.shape), lspec(convw.shape),
                  lspec(ppool.shape), lspec(pconv.shape)],
        out_specs=(pl.BlockSpec((1, TM, D), row), pl.BlockSpec((1, TM, D), row),
                   pl.BlockSpec((1, A, TM), tr), pl.BlockSpec((1, TM, 2 * A), row),
                   pl.BlockSpec((1, ATTN_HEADS * V_ROWS, TM), tr),
                   pl.BlockSpec((1, 1, TM // MOBA_BLOCK, A), lambda b, s: (b, s, 0, 0))),
        out_shape=(jax.ShapeDtypeStruct((B, S, D), BF16), jax.ShapeDtypeStruct((B, S, D), BF16),
                   jax.ShapeDtypeStruct((B, A, S), BF16), jax.ShapeDtypeStruct((B, S, 2 * A), BF16),
                   jax.ShapeDtypeStruct((B, ATTN_HEADS * V_ROWS, S), BF16),
                   jax.ShapeDtypeStruct((B, nt, TM // MOBA_BLOCK, A), F32)),
        scratch_shapes=[pltpu.VMEM((SUB + HALO, pool_w + conv_w), F32)],
        compiler_params=_params(2),
        name="in_proj_mixers",
    )(x, mod, n1w, w_in, qw, kw, cos_t, sin_t, pwbd, pscale, convw, ppool, pconv)


def _attn_kernel(qT_ref, kx_ref, vx_ref, km_ref, *refs, n_blocks, n_cast):
    o_ref = refs[n_cast]
    for w_ref, wb_ref in zip(refs[:n_cast], refs[n_cast + 1:]):
        wb_ref[...] = w_ref[...].astype(BF16)

    TQ = MOBA_BLOCK
    heads_per_pair = PAIR // HEAD_DIM
    km = km_ref[0]
    blk = lax.broadcasted_iota(jnp.int32, (n_blocks, TQ), 0)
    kpos = lax.broadcasted_iota(jnp.int32, (TQ, TQ), 0)
    qpos = lax.broadcasted_iota(jnp.int32, (TQ, TQ), 1)
    causal = kpos <= qpos
    half = lax.broadcasted_iota(jnp.int32, (PAIR, TQ), 0) // HEAD_DIM
    bias_pad = jnp.zeros((BF16_SUBLANES - n_blocks, TQ), F32)
    rhs_pad = jnp.zeros((PAIR - BF16_SUBLANES, TQ), BF16)

    def make_rhs(qi, h):
        p, hh = divmod(h, heads_per_pair)
        lanes = slice(p * PAIR, (p + 1) * PAIR)
        qT_pair = qT_ref[0, lanes, qi * TQ:(qi + 1) * TQ]
        qT_h = jnp.where(half == hh, qT_pair, jnp.zeros_like(qT_pair))
        if qi <= MOBA_TOPK:
            sel = blk <= qi
        else:
            past = blk < qi
            km_pair = km[:, lanes]
            km_hi = km_pair.astype(BF16)
            km_lo = (km_pair - km_hi.astype(F32)).astype(BF16)
            gate = _dot(km_hi, qT_h) + _dot(km_lo, qT_h)
            gate = jnp.where(past, gate, NEG)
            beaten = jnp.zeros((n_blocks, TQ), jnp.int32)
            for m in range(n_blocks):
                gm = gate[m:m + 1, :]
                ahead = (gm > gate) | ((gm == gate) & (m < blk))
                beaten = beaten + ahead.astype(jnp.int32)
            sel = (past & (beaten < MOBA_TOPK)) | (blk == qi)
        bias = jnp.concatenate([jnp.where(sel, 0.0, NEG), bias_pad], axis=0).astype(BF16)
        return jnp.concatenate([qT_h, bias, rhs_pad], axis=0)

    items = [(qi, n, h) for qi in range(n_blocks) for n in [qi] + list(range(qi)) for h in range(ATTN_HEADS)]
    rhs = {}
    run_max = {}
    run_acc = {}
    done = {}

    def scores(item):
        qi, n, h = item
        if (qi, h) not in rhs:
            rhs[(qi, h)] = make_rhs(qi, h)
        p = h // heads_per_pair
        s = _dot(kx_ref[0, n * TQ:(n + 1) * TQ, 2 * p * PAIR:(2 * p + 2) * PAIR], rhs[(qi, h)])
        return s.astype(BF16)

    def values(qi, n, h, pr, alpha):
        acc = _dot(vx_ref[0, h * V_ROWS:(h + 1) * V_ROWS, n * TQ:(n + 1) * TQ], pr)
        if alpha is not None:
            acc = alpha * run_acc[(qi, h)] + acc
        run_acc[(qi, h)] = acc
        last_block = qi - 1 if qi > 0 else 0
        if n == last_block:
            done[h] = acc[0:HEAD_DIM, :] / acc[HEAD_DIM:HEAD_DIM + 1, :]
            del run_acc[(qi, h)], run_max[(qi, h)], rhs[(qi, h)]
            if h == ATTN_HEADS - 1:
                oT = jnp.concatenate([done[j] for j in range(ATTN_HEADS)], axis=0)
                o_ref[0, qi * TQ:(qi + 1) * TQ, :] = oT.T.astype(BF16)

    pending = [scores(it) for it in items[:QK_AHEAD]]
    late = []
    for i, (qi, n, h) in enumerate(items):
        if i + QK_AHEAD < len(items):
            pending.append(scores(items[i + QK_AHEAD]))
        s = pending.pop(0)
        if n == qi:
            s = jnp.where(causal, s, jnp.asarray(NEG, BF16))
        m_n = jnp.max(s, axis=0, keepdims=True).astype(F32)
        alpha = None
        if (qi, h) in run_max:
            m_n = jnp.maximum(run_max[(qi, h)], m_n)
            alpha = jnp.exp2(run_max[(qi, h)] - m_n)
        run_max[(qi, h)] = m_n
        late.append((qi, n, h, jnp.exp2(s - m_n.astype(BF16)), alpha))
        if len(late) > PV_BEHIND:
            values(*late.pop(0))
    for job in late:
        values(*job)


def _attn_call(qT, kx, vx, km, casts):
    B, A, S = qT.shape
    nb = S // MOBA_BLOCK
    assert nb <= BF16_SUBLANES and nb <= PAIR
    kern = functools.partial(_attn_kernel, n_blocks=nb, n_cast=len(casts))
    whole = lambda b: (b, 0, 0)
    cast_in, cast_out, cast_shape = [], [], []
    for w, layer in casts:
        _, rows, cols = w.shape
        assert rows % (B * BF16_SUBLANES) == 0
        cast_in.append(pl.BlockSpec((None, rows // B, cols), lambda b, layer=layer: (layer, b, 0)))
        cast_out.append(pl.BlockSpec((rows // B, cols), lambda b: (b, 0)))
        cast_shape.append(jax.ShapeDtypeStruct((rows, cols), BF16))
    out = pl.pallas_call(
        kern,
        grid=(B,),
        in_specs=[pl.BlockSpec((1, A, S), whole),
                  pl.BlockSpec((1, S, 2 * A), whole),
                  pl.BlockSpec((1, ATTN_HEADS * V_ROWS, S), whole),
                  pl.BlockSpec((1, nb, A), whole)] + cast_in,
        out_specs=[pl.BlockSpec((1, S, A), whole)] + cast_out,
        out_shape=[jax.ShapeDtypeStruct((B, S, A), BF16)] + cast_shape,
        compiler_params=_params(1),
        name="moba_attention",
    )(qT, kx, vx, km, *[w for w, _ in casts])
    return out[0], out[1:]


def _post_kernel(x_ref, part_ref, ga_ref, ya_ref, mod_ref, n2w_ref,
                 pattn_ref, wout_ref, wg_ref, wu_ref, wd_ref, o_ref, act_ref):
    g1 = mod_ref[2:3, :]
    sh2 = mod_ref[3:4, :]
    sc2 = mod_ref[4:5, :]
    g2 = mod_ref[5:6, :]
    merged = part_ref[0].astype(F32) + ga_ref[0].astype(F32) * _dot(ya_ref[0], pattn_ref[...])
    x1 = x_ref[0] + g1 * _dot(merged.astype(BF16), wout_ref[...])
    ms = jnp.mean(x1 * x1, axis=-1, keepdims=True)
    h2 = (x1 * lax.rsqrt(ms + EPS)) * (n2w_ref[...] * (1.0 + sc2)) + sh2
    h2 = h2.astype(BF16)
    d_ff = wg_ref.shape[1]

    def ff_dots(j):
        cols = slice(j * FF_TN, (j + 1) * FF_TN)
        return _dot(h2, wg_ref[:, cols]), _dot(h2, wu_ref[:, cols])

    res = ff_dots(0)
    for j in range(d_ff // FF_TN):
        nxt = ff_dots(j + 1) if (j + 1) * FF_TN < d_ff else None
        g, u = res
        act_ref[:, j * FF_TN:(j + 1) * FF_TN] = (g * jax.nn.sigmoid(g) * u).astype(BF16)
        res = nxt
    o_ref[0] = x1 + g2 * _dot(act_ref[...], wd_ref[...])


def _post_call(layer, x, part, ga, ya, mod, n2w, pattn, wout, wg, wu, wd):
    B, S, D = x.shape
    A = ya.shape[-1]
    d_ff = wg.shape[-1]
    assert d_ff % FF_TN == 0
    row = lambda b, s: (b, s, 0)
    lspec = functools.partial(_layer_spec, layer)
    return pl.pallas_call(
        _post_kernel,
        grid=(B, S // TM),
        in_specs=[pl.BlockSpec((1, TM, D), row), pl.BlockSpec((1, TM, D), row),
                  pl.BlockSpec((1, TM, D), row), pl.BlockSpec((1, TM, A), row),
                  pl.BlockSpec((None, None) + mod.shape[2:], lambda b, s: (layer, b, 0, 0)),
                  lspec(n2w.shape),
                  lspec(pattn.shape), lspec(wout.shape), lspec(wg.shape), lspec(wu.shape), lspec(wd.shape)],
        out_specs=pl.BlockSpec((1, TM, D), row),
        out_shape=jax.ShapeDtypeStruct((B, S, D), F32),
        scratch_shapes=[pltpu.VMEM((TM, d_ff), BF16)],
        compiler_params=_params(2),
        name="merge_swiglu",
    )(x, part, ga, ya, mod, n2w, pattn, wout, wg, wu, wd)


def kernel(x, c, positions, norm1_w, norm2_w, w_ada, b_ada, w_in, pool_w, pool_scale, conv_w,
           q_norm_w, k_norm_w, p_pool, p_conv, p_attn, w_out, w_gate, w_up, w_down):
    B, S, D = x.shape
    L = w_ada.shape[0]
    assert S % TM == 0 and TM % SUB == 0 and SUB % MOBA_BLOCK == 0 and HALO >= max(POOL_WINDOWS) - 1
    n_groups, gdim = pool_w.shape[1], pool_w.shape[2]
    assert n_groups == len(POOL_WINDOWS) and conv_w.shape[1] == CONV_K

    mod = _modulation(c, w_ada, b_ada).reshape(L, B, 6, D)
    cos_t, sin_t = _rope_tables(positions)

    pwbd = jnp.zeros((L, n_groups * gdim, n_groups * gdim), F32)
    for g in range(n_groups):
        pwbd = pwbd.at[:, g * gdim:(g + 1) * gdim, g * gdim:(g + 1) * gdim].set(pool_w[:, g])
    pwbd = pwbd.astype(BF16)
    qw = q_norm_w.reshape(L, HEAD_DIM, 1)
    kw = k_norm_w.reshape(L, HEAD_DIM, 1)
    n1w = norm1_w.reshape(L, 1, D)
    n2w = norm2_w.reshape(L, 1, D)
    pscale = pool_scale.reshape(L, 1, -1)

    in_f32 = (w_in, p_pool, p_conv)
    post_f32 = (p_attn, w_out, w_gate, w_up, w_down)
    in_bf = [w[0].astype(BF16) for w in in_f32]

    for l in range(L):
        part, ga, qT, kx, vx, km = _in_call(l, x, mod, n1w, in_bf[0], qw, kw, cos_t, sin_t, pwbd,
                                            pscale, conv_w, in_bf[1], in_bf[2])
        casts = [(w, l) for w in post_f32] + ([(w, l + 1) for w in in_f32] if l + 1 < L else [])
        ya, cast = _attn_call(qT, kx, vx, km.reshape(B, S // MOBA_BLOCK, ATTN_WIDTH), casts)
        x = _post_call(l, x, part, ga, ya, mod, n2w, *cast[:len(post_f32)])
        in_bf = cast[len(post_f32):]
    return x
```

```python
import functools
import math

import numpy as np
import jax
import jax.numpy as jnp
from jax import lax
from jax.experimental import pallas as pl
from jax.experimental.pallas import tpu as pltpu

F32 = jnp.float32
BF16 = jnp.bfloat16

HEAD_DIM = 64
ATTN_HEADS = 8
ATTN_WIDTH = ATTN_HEADS * HEAD_DIM
ROPE_DIM = HEAD_DIM // 4
ROPE_HALF = ROPE_DIM // 2
ROPE_THETA = 500000.0
MOBA_BLOCK = 256
MOBA_TOPK = 3
POOL_WINDOWS = (2, 4, 8, 16)
CONV_K = 3
EPS = 1e-6
NEG = -1e30

LANES = 128
PAIR = LANES
HALO = 16
VMEM_LIMIT = 56 * 1024 * 1024

TM = 512
SUB = TM
FF_TN = 256
GATE_TN = 512
PV_BEHIND = 1
QK_AHEAD = 5
BF16_SUBLANES = 16
V_ROWS = HEAD_DIM + BF16_SUBLANES
Q_SCALE = (1.0 / math.sqrt(HEAD_DIM)) * math.log2(math.e)


def _dot(a, b):
    return jnp.dot(a, b, preferred_element_type=F32)


def _const_spec(shape):
    nd = len(shape)
    return pl.BlockSpec(shape, lambda *_: (0,) * nd, pipeline_mode=pl.Buffered(1))


def _layer_spec(layer, shape):
    if len(shape) == 2:
        return _const_spec(shape)
    nd = len(shape)
    return pl.BlockSpec((None,) + tuple(shape[1:]), lambda *_: (layer,) + (0,) * (nd - 1),
                        pipeline_mode=pl.Buffered(1))


def _params(n_axes):
    return pltpu.CompilerParams(dimension_semantics=("arbitrary",) * n_axes,
                                vmem_limit_bytes=VMEM_LIMIT)


N_MOD = 6


def _mod_chunk(c_ref, w_ref, b_ref, o_ref):
    c = c_ref[...]
    c_act = (c * jax.nn.sigmoid(c)).astype(BF16)
    o_ref[...] = _dot(c_act, w_ref[...].astype(BF16)) + b_ref[...]


def _mod_specs(layer, n_batch, d_model, step_to_chunk):
    ins = [pl.BlockSpec((n_batch, d_model), lambda i: (0, 0)),
           pl.BlockSpec((None, d_model, d_model), lambda i: (layer, 0, step_to_chunk(i))),
           pl.BlockSpec((None, 1, d_model), lambda i: (layer, 0, step_to_chunk(i)))]
    out = pl.BlockSpec((None, n_batch, d_model), lambda i: (step_to_chunk(i), 0, 0))
    shape = jax.ShapeDtypeStruct((N_MOD, n_batch, d_model), F32)
    return ins, out, shape


def _modulation(layer, c, w_ada, b_ada3):
    B, D = c.shape
    ins, out, shape = _mod_specs(layer, B, D, lambda i: i)
    return pl.pallas_call(
        _mod_chunk, grid=(N_MOD,), in_specs=ins, out_specs=out, out_shape=shape,
        compiler_params=_params(1), name="adaln_modulation",
    )(c, w_ada, b_ada3)


def _rope_kernel(pos_ref, freq_ref, c_ref, s_ref):
    ang = freq_ref[...] * pos_ref[0].astype(F32)
    c_ref[0] = jnp.cos(ang)
    s_ref[0] = jnp.sin(ang)


def _rope_tables(positions):
    B, S = positions.shape
    freq = ROPE_THETA ** (-np.arange(0, ROPE_DIM, 2, dtype=np.float64) / ROPE_DIM)
    freq = jnp.asarray(freq.reshape(ROPE_HALF, 1), F32)
    spec = pl.BlockSpec((1, ROPE_HALF, S), lambda b: (b, 0, 0))
    shp = jax.ShapeDtypeStruct((B, ROPE_HALF, S), F32)
    return pl.pallas_call(
        _rope_kernel,
        grid=(B,),
        in_specs=[pl.BlockSpec((1, 1, S), lambda b: (b, 0, 0)),
                  pl.BlockSpec((ROPE_HALF, 1), lambda b: (0, 0))],
        out_specs=(spec, spec),
        out_shape=(shp, shp),
        compiler_params=_params(1),
        name="rotary_tables",
    )(positions.reshape(B, 1, S), freq)


def _in_kernel(x_ref, mod_ref, n1w_ref, win_ref, qw_ref, kw_ref,
               cos_ref, sin_ref, pwbd_ref, pscale_ref, convw_ref, ppool_ref, pconv_ref,
               part_ref, ga_ref, qT_ref, kx_ref, vx_ref, km_ref, ext_ref,
               *, d_model, pool_w, conv_w, attn_w):
    si = pl.program_id(1)
    n_sub = TM // SUB
    hs = {}

    def adaln(sub):
        x = x_ref[0, sub * SUB:(sub + 1) * SUB, :]
        ms = jnp.mean(x * x, axis=-1, keepdims=True)
        sh1 = mod_ref[0, pl.ds(pl.program_id(0), 1), :]
        sc1 = mod_ref[1, pl.ds(pl.program_id(0), 1), :]
        hs[sub] = ((x * lax.rsqrt(ms + EPS)) * (n1w_ref[...] * (1.0 + sc1)) + sh1).astype(BF16)

    @pl.when(si == 0)
    def _():
        ext_ref[0:HALO, :] = jnp.zeros((HALO, pool_w + conv_w), F32)
        km_ref[0] = jnp.zeros(km_ref.shape[1:], F32)

    o_pool = 0
    o_uc = o_pool + pool_w
    o_bc = o_uc + conv_w
    o_cc = o_bc + conv_w
    o_q = o_cc + conv_w
    o_k = o_q + attn_w
    o_v = o_k + attn_w
    o_g = o_v + attn_w
    n_pairs = attn_w // PAIR

    def sub_stages(sub):
        rows = slice(sub * SUB, (sub + 1) * SUB)
        blk0 = sub * (SUB // MOBA_BLOCK)
        cos_t = cos_ref[0, :, rows]
        sin_t = sin_ref[0, :, rows]

        def norm_rope_t(u, gain_ref):
            gain = gain_ref[...]
            outs = []
            for p in range(n_pairs):
                u_t = u[:, p * PAIR:(p + 1) * PAIR].T
                parts = []
                for hh in range(PAIR // HEAD_DIM):
                    xh = u_t[hh * HEAD_DIM:(hh + 1) * HEAD_DIM, :]
                    xn = xh * lax.rsqrt(jnp.mean(xh * xh, axis=0, keepdims=True) + EPS) * gain
                    x1 = xn[0:ROPE_HALF, :]
                    x2 = xn[ROPE_HALF:ROPE_DIM, :]
                    parts += [x1 * cos_t - x2 * sin_t, x2 * cos_t + x1 * sin_t, xn[ROPE_DIM:, :]]
                outs.append(jnp.concatenate(parts, axis=0))
            return outs

        def proj(lo, hi):
            return lambda: (_dot(hs[sub], win_ref[:, lo:hi]),)

        def q_stage(uq):
            for p, qp_t in enumerate(norm_rope_t(uq, qw_ref)):
                qT_ref[0, p * PAIR:(p + 1) * PAIR, rows] = (qp_t * Q_SCALE).astype(BF16)

        def k_stage(uk):
            row_blk = (si * (TM // MOBA_BLOCK) + blk0
                       + lax.broadcasted_iota(jnp.int32, (SUB, PAIR), 0) // MOBA_BLOCK)
            onehot = jnp.where(lax.broadcasted_iota(jnp.int32, (SUB, PAIR), 1) == row_blk, 1.0, 0.0).astype(BF16)
            for p, kp_t in enumerate(norm_rope_t(uk, kw_ref)):
                kp = kp_t.T
                kx_ref[0, rows, 2 * p * PAIR:(2 * p + 1) * PAIR] = kp.astype(BF16)
                kx_ref[0, rows, (2 * p + 1) * PAIR:(2 * p + 2) * PAIR] = onehot
                for j in range(SUB // MOBA_BLOCK):
                    lanes = slice(p * PAIR, (p + 1) * PAIR)
                    mean = jnp.mean(kp[j * MOBA_BLOCK:(j + 1) * MOBA_BLOCK, :], axis=0, keepdims=True)
                    blk_row = lax.broadcasted_iota(jnp.int32, (km_ref.shape[1], PAIR), 0)
                    mine = blk_row == si * (TM // MOBA_BLOCK) + blk0 + j
                    km_ref[0, :, lanes] = jnp.where(mine, mean, km_ref[0, :, lanes])

        def v_stage(uv):
            ones_rows = jnp.where(lax.broadcasted_iota(jnp.int32, (V_ROWS - HEAD_DIM, SUB), 0) == 0,
                                  1.0, 0.0).astype(BF16)
            for p in range(n_pairs):
                vT = uv[:, p * PAIR:(p + 1) * PAIR].T.astype(BF16)
                for hh in range(PAIR // HEAD_DIM):
                    r0 = (p * (PAIR // HEAD_DIM) + hh) * V_ROWS
                    vx_ref[0, r0:r0 + HEAD_DIM, rows] = vT[hh * HEAD_DIM:(hh + 1) * HEAD_DIM, :]
                    vx_ref[0, r0 + HEAD_DIM:r0 + V_ROWS, rows] = ones_rows

        mixed = {}

        def mixer_stage(ua):
            up = ua[:, o_pool:o_uc]
            uc = ua[:, o_uc:o_bc]
            bc = ua[:, o_bc:o_cc]
            cc = ua[:, o_cc:o_q]
            ext_ref[HALO:HALO + SUB, 0:pool_w] = up
            ext_ref[HALO:HALO + SUB, pool_w:pool_w + conv_w] = cc * uc
            ext = ext_ref[...]
            ext_ref[0:HALO, :] = ext[SUB:SUB + HALO, :]
            e_p = ext[:, 0:pool_w]
            e_c = ext[:, pool_w:pool_w + conv_w]

            s2 = e_p + pltpu.roll(e_p, 1, 0)
            s4 = s2 + pltpu.roll(s2, 2, 0)
            s8 = s4 + pltpu.roll(s4, 4, 0)
            s16 = s8 + pltpu.roll(s8, 8, 0)
            gdim = pool_w // len(POOL_WINDOWS)
            lane = lax.broadcasted_iota(jnp.int32, (1, pool_w), 1)
            wsum = jnp.where(lane < gdim, s2, jnp.where(lane < 2 * gdim, s4, jnp.where(lane < 3 * gdim, s8, s16)))
            win = jnp.where(lane < gdim, POOL_WINDOWS[0],
                            jnp.where(lane < 2 * gdim, POOL_WINDOWS[1],
                                      jnp.where(lane < 3 * gdim, POOL_WINDOWS[2], POOL_WINDOWS[3])))
            t = si * TM + sub * SUB + lax.broadcasted_iota(jnp.int32, (SUB, pool_w), 0)
            cnt = jnp.minimum(t + 1, win).astype(F32)
            pooled = wsum[HALO:, :] / cnt - up
            yp = _dot(pooled.astype(BF16), pwbd_ref[...]) * pscale_ref[...]
            mixed["pool"] = _dot(yp.astype(BF16), ppool_ref[...])

            cw = convw_ref[...]
            conv = cw[0:1, :] * pltpu.roll(e_c, 2, 0) + cw[1:2, :] * pltpu.roll(e_c, 1, 0) + cw[2:3, :] * e_c
            yc = bc * conv[HALO:, :]
            mixed["conv"] = _dot(yc.astype(BF16), pconv_ref[...])

        def gate_dots(c):
            cols = [slice(o_g + i * d_model + c * GATE_TN, o_g + i * d_model + (c + 1) * GATE_TN) for i in range(3)]
            return lambda: tuple(_dot(hs[sub], win_ref[:, cs]) for cs in cols)

        def gate_stage(c):
            out = slice(c * GATE_TN, (c + 1) * GATE_TN)

            def run(g0, g1, g2):
                merged = jax.nn.sigmoid(g0) * mixed["pool"][:, out] + jax.nn.sigmoid(g1) * mixed["conv"][:, out]
                part_ref[0, rows, out] = merged.astype(BF16)
                ga_ref[0, rows, out] = jax.nn.sigmoid(g2).astype(BF16)
            return run

        stages = [(proj(o_q, o_k), q_stage), (proj(o_k, o_v), k_stage)]
        if sub + 1 < n_sub:
            stages.append((lambda: (), lambda: adaln(sub + 1)))
        stages += [(proj(o_v, o_g), v_stage), (proj(o_pool, o_q), mixer_stage)]
        stages += [(gate_dots(c), gate_stage(c)) for c in range(d_model // GATE_TN)]
        return stages

    adaln(0)
    stages = [st for sub in range(n_sub) for st in sub_stages(sub)]
    res = stages[0][0]()
    for i, (_, finish) in enumerate(stages):
        nxt = stages[i + 1][0]() if i + 1 < len(stages) else None
        finish(*res)
        res = nxt


def _in_call(layer, x, mod, n1w, w_in, qw, kw, cos_t, sin_t, pwbd, pscale, convw, ppool, pconv):
    B, S, D = x.shape
    pool_w = pwbd.shape[-1]
    conv_w = convw.shape[-1]
    A = ATTN_WIDTH
    nt = S // TM
    row = lambda b, s: (b, s, 0)
    tr = lambda b, s: (b, 0, s)
    lspec = functools.partial(_layer_spec, layer)
    kern = functools.partial(_in_kernel, d_model=D, pool_w=pool_w, conv_w=conv_w, attn_w=A)
    return pl.pallas_call(
        kern,
        grid=(B, nt),
        in_specs=[pl.BlockSpec((1, TM, D), row),
                  _const_spec(mod.shape),
                  lspec(n1w.shape), lspec(w_in.shape),
                  lspec(qw.shape), lspec(kw.shape),
                  pl.BlockSpec((1, ROPE_HALF, TM), tr), pl.BlockSpec((1, ROPE_HALF, TM), tr),
                  lspec(pwbd.shape), lspec(pscale.shape), lspec(convw.shape),
                  lspec(ppool.shape), lspec(pconv.shape)],
        out_specs=(pl.BlockSpec((1, TM, D), row), pl.BlockSpec((1, TM, D), row),
                   pl.BlockSpec((1, A, TM), tr), pl.BlockSpec((1, TM, 2 * A), row),
                   pl.BlockSpec((1, ATTN_HEADS * V_ROWS, TM), tr),
                   pl.BlockSpec((1, S // MOBA_BLOCK, A), lambda b, s: (b, 0, 0))),
        out_shape=(jax.ShapeDtypeStruct((B, S, D), BF16), jax.ShapeDtypeStruct((B, S, D), BF16),
                   jax.ShapeDtypeStruct((B, A, S), BF16), jax.ShapeDtypeStruct((B, S, 2 * A), BF16),
                   jax.ShapeDtypeStruct((B, ATTN_HEADS * V_ROWS, S), BF16),
                   jax.ShapeDtypeStruct((B, S // MOBA_BLOCK, A), F32)),
        scratch_shapes=[pltpu.VMEM((SUB + HALO, pool_w + conv_w), F32)],
        compiler_params=_params(2),
        name="in_proj_mixers",
    )(x, mod, n1w, w_in, qw, kw, cos_t, sin_t, pwbd, pscale, convw, ppool, pconv)


def _attn_kernel(qT_ref, kx_ref, vx_ref, km_ref, *refs, n_blocks, n_cast, with_mod):
    n_in = n_cast + (3 if with_mod else 0)
    o_ref = refs[n_in]
    for w_ref, wb_ref in zip(refs[:n_cast], refs[n_in + 1:n_in + 1 + n_cast]):
        wb_ref[...] = w_ref[...].astype(BF16)
    if with_mod:
        _mod_chunk(*refs[n_cast:n_in], refs[n_in + 1 + n_cast])

    TQ = MOBA_BLOCK
    heads_per_pair = PAIR // HEAD_DIM
    km = km_ref[0]
    blk = lax.broadcasted_iota(jnp.int32, (n_blocks, TQ), 0)
    kpos = lax.broadcasted_iota(jnp.int32, (TQ, TQ), 0)
    qpos = lax.broadcasted_iota(jnp.int32, (TQ, TQ), 1)
    causal = kpos <= qpos
    half = lax.broadcasted_iota(jnp.int32, (PAIR, TQ), 0) // HEAD_DIM
    bias_pad = jnp.zeros((BF16_SUBLANES - n_blocks, TQ), F32)
    rhs_pad = jnp.zeros((PAIR - BF16_SUBLANES, TQ), BF16)

    def make_rhs(qi, h):
        p, hh = divmod(h, heads_per_pair)
        lanes = slice(p * PAIR, (p + 1) * PAIR)
        qT_pair = qT_ref[0, lanes, qi * TQ:(qi + 1) * TQ]
        qT_h = jnp.where(half == hh, qT_pair, jnp.zeros_like(qT_pair))
        if qi <= MOBA_TOPK:
            sel = blk <= qi
        else:
            past = blk < qi
            km_pair = km[:, lanes]
            km_hi = km_pair.astype(BF16)
            km_lo = (km_pair - km_hi.astype(F32)).astype(BF16)
            gate = _dot(km_hi, qT_h) + _dot(km_lo, qT_h)
            gate = jnp.where(past, gate, NEG)
            beaten = jnp.zeros((n_blocks, TQ), jnp.int32)
            for m in range(n_blocks):
                gm = gate[m:m + 1, :]
                ahead = (gm > gate) | ((gm == gate) & (m < blk))
                beaten = beaten + ahead.astype(jnp.int32)
            sel = (past & (beaten < MOBA_TOPK)) | (blk == qi)
        bias = jnp.concatenate([jnp.where(sel, 0.0, NEG), bias_pad], axis=0).astype(BF16)
        return jnp.concatenate([qT_h, bias, rhs_pad], axis=0)

    items = [(qi, n, h) for qi in range(n_blocks) for n in [qi] + list(range(qi)) for h in range(ATTN_HEADS)]
    rhs = {}
    run_max = {}
    run_acc = {}
    done = {}

    def scores(item):
        qi, n, h = item
        if (qi, h) not in rhs:
            rhs[(qi, h)] = make_rhs(qi, h)
        p = h // heads_per_pair
        s = _dot(kx_ref[0, n * TQ:(n + 1) * TQ, 2 * p * PAIR:(2 * p + 2) * PAIR], rhs[(qi, h)])
        return s.astype(BF16)

    def values(qi, n, h, pr, alpha):
        acc = _dot(vx_ref[0, h * V_ROWS:(h + 1) * V_ROWS, n * TQ:(n + 1) * TQ], pr)
        if alpha is not None:
            acc = alpha * run_acc[(qi, h)] + acc
        run_acc[(qi, h)] = acc
        last_block = qi - 1 if qi > 0 else 0
        if n == last_block:
            done[h] = acc[0:HEAD_DIM, :] / acc[HEAD_DIM:HEAD_DIM + 1, :]
            del run_acc[(qi, h)], run_max[(qi, h)], rhs[(qi, h)]
            if h == ATTN_HEADS - 1:
                oT = jnp.concatenate([done[j] for j in range(ATTN_HEADS)], axis=0)
                o_ref[0, qi * TQ:(qi + 1) * TQ, :] = oT.T.astype(BF16)

    pending = [scores(it) for it in items[:QK_AHEAD]]
    late = []
    for i, (qi, n, h) in enumerate(items):
        if i + QK_AHEAD < len(items):
            pending.append(scores(items[i + QK_AHEAD]))
        s = pending.pop(0)
        if n == qi:
            s = jnp.where(causal, s, jnp.asarray(NEG, BF16))
        m_n = jnp.max(s, axis=0, keepdims=True).astype(F32)
        alpha = None
        if (qi, h) in run_max:
            m_n = jnp.maximum(run_max[(qi, h)], m_n)
            alpha = jnp.exp2(run_max[(qi, h)] - m_n)
        run_max[(qi, h)] = m_n
        late.append((qi, n, h, jnp.exp2(s - m_n.astype(BF16)), alpha))
        if len(late) > PV_BEHIND:
            values(*late.pop(0))
    for job in late:
        values(*job)


def _attn_call(qT, kx, vx, km, casts, mod_job):
    B, A, S = qT.shape
    nb = S // MOBA_BLOCK
    assert nb <= BF16_SUBLANES and nb <= PAIR
    kern = functools.partial(_attn_kernel, n_blocks=nb, n_cast=len(casts), with_mod=mod_job is not None)
    whole = lambda b: (b, 0, 0)
    mod_in, mod_out, mod_shape, mod_args = [], [], [], []
    if mod_job is not None:
        layer, c, w_ada, b_ada3 = mod_job
        assert B >= N_MOD
        mod_in, out, shape = _mod_specs(layer, B, c.shape[1], lambda i: jnp.minimum(i, N_MOD - 1))
        mod_out, mod_shape, mod_args = [out], [shape], [c, w_ada, b_ada3]
    cast_in, cast_out, cast_shape = [], [], []
    for w, layer in casts:
        _, rows, cols = w.shape
        assert rows % (B * BF16_SUBLANES) == 0
        cast_in.append(pl.BlockSpec((None, rows // B, cols), lambda b, layer=layer: (layer, b, 0)))
        cast_out.append(pl.BlockSpec((rows // B, cols), lambda b: (b, 0)))
        cast_shape.append(jax.ShapeDtypeStruct((rows, cols), BF16))
    out = pl.pallas_call(
        kern,
        grid=(B,),
        in_specs=[pl.BlockSpec((1, A, S), whole),
                  pl.BlockSpec((1, S, 2 * A), whole),
                  pl.BlockSpec((1, ATTN_HEADS * V_ROWS, S), whole),
                  pl.BlockSpec((1, nb, A), whole)] + cast_in + mod_in,
        out_specs=[pl.BlockSpec((1, S, A), whole)] + cast_out + mod_out,
        out_shape=[jax.ShapeDtypeStruct((B, S, A), BF16)] + cast_shape + mod_shape,
        compiler_params=_params(1),
        name="moba_attention",
    )(qT, kx, vx, km, *[w for w, _ in casts], *mod_args)
    n_cast = len(casts)
    return out[0], out[1:1 + n_cast], (out[1 + n_cast] if mod_job is not None else None)


def _post_kernel(x_ref, part_ref, ga_ref, ya_ref, mod_ref, n2w_ref,
                 pattn_ref, wout_ref, wg_ref, wu_ref, wd_ref, o_ref, act_ref):
    row = pl.ds(pl.program_id(0), 1)
    g1 = mod_ref[2, row, :]
    sh2 = mod_ref[3, row, :]
    sc2 = mod_ref[4, row, :]
    g2 = mod_ref[5, row, :]
    merged = part_ref[0].astype(F32) + ga_ref[0].astype(F32) * _dot(ya_ref[0], pattn_ref[...])
    x1 = x_ref[0] + g1 * _dot(merged.astype(BF16), wout_ref[...])
    ms = jnp.mean(x1 * x1, axis=-1, keepdims=True)
    h2 = (x1 * lax.rsqrt(ms + EPS)) * (n2w_ref[...] * (1.0 + sc2)) + sh2
    h2 = h2.astype(BF16)
    d_ff = wg_ref.shape[1]

    def ff_dots(j):
        cols = slice(j * FF_TN, (j + 1) * FF_TN)
        return _dot(h2, wg_ref[:, cols]), _dot(h2, wu_ref[:, cols])

    res = ff_dots(0)
    for j in range(d_ff // FF_TN):
        nxt = ff_dots(j + 1) if (j + 1) * FF_TN < d_ff else None
        g, u = res
        act_ref[:, j * FF_TN:(j + 1) * FF_TN] = (g * jax.nn.sigmoid(g) * u).astype(BF16)
        res = nxt
    o_ref[0] = x1 + g2 * _dot(act_ref[...], wd_ref[...])


def _post_call(layer, x, part, ga, ya, mod, n2w, pattn, wout, wg, wu, wd):
    B, S, D = x.shape
    A = ya.shape[-1]
    d_ff = wg.shape[-1]
    assert d_ff % FF_TN == 0
    row = lambda b, s: (b, s, 0)
    lspec = functools.partial(_layer_spec, layer)
    return pl.pallas_call(
        _post_kernel,
        grid=(B, S // TM),
        in_specs=[pl.BlockSpec((1, TM, D), row), pl.BlockSpec((1, TM, D), row),
                  pl.BlockSpec((1, TM, D), row), pl.BlockSpec((1, TM, A), row),
                  _const_spec(mod.shape),
                  lspec(n2w.shape),
                  lspec(pattn.shape), lspec(wout.shape), lspec(wg.shape), lspec(wu.shape), lspec(wd.shape)],
        out_specs=pl.BlockSpec((1, TM, D), row),
        out_shape=jax.ShapeDtypeStruct((B, S, D), F32),
        scratch_shapes=[pltpu.VMEM((TM, d_ff), BF16)],
        compiler_params=_params(2),
        name="merge_swiglu",
    )(x, part, ga, ya, mod, n2w, pattn, wout, wg, wu, wd)


def kernel(x, c, positions, norm1_w, norm2_w, w_ada, b_ada, w_in, pool_w, pool_scale, conv_w,
           q_norm_w, k_norm_w, p_pool, p_conv, p_attn, w_out, w_gate, w_up, w_down):
    B, S, D = x.shape
    L = w_ada.shape[0]
    assert S % TM == 0 and TM % SUB == 0 and SUB % MOBA_BLOCK == 0 and HALO >= max(POOL_WINDOWS) - 1
    n_groups, gdim = pool_w.shape[1], pool_w.shape[2]
    assert n_groups == len(POOL_WINDOWS) and conv_w.shape[1] == CONV_K

    assert w_ada.shape[2] == N_MOD * D
    b_ada3 = b_ada.reshape(L, 1, N_MOD * D)
    mod = _modulation(0, c, w_ada, b_ada3)
    cos_t, sin_t = _rope_tables(positions)

    pwbd = jnp.zeros((L, n_groups * gdim, n_groups * gdim), F32)
    for g in range(n_groups):
        pwbd = pwbd.at[:, g * gdim:(g + 1) * gdim, g * gdim:(g + 1) * gdim].set(pool_w[:, g])
    pwbd = pwbd.astype(BF16)
    qw = q_norm_w.reshape(L, HEAD_DIM, 1)
    kw = k_norm_w.reshape(L, HEAD_DIM, 1)
    n1w = norm1_w.reshape(L, 1, D)
    n2w = norm2_w.reshape(L, 1, D)
    pscale = pool_scale.reshape(L, 1, -1)

    in_f32 = (w_in, p_pool, p_conv)
    post_f32 = (p_attn, w_out, w_gate, w_up, w_down)
    in_bf = [w[0].astype(BF16) for w in in_f32]

    for l in range(L):
        part, ga, qT, kx, vx, km = _in_call(l, x, mod, n1w, in_bf[0], qw, kw, cos_t, sin_t, pwbd,
                                            pscale, conv_w, in_bf[1], in_bf[2])
        more = l + 1 < L
        casts = [(w, l) for w in post_f32] + ([(w, l + 1) for w in in_f32] if more else [])
        ya, cast, mod_next = _attn_call(qT, kx, vx, km, casts, (l + 1, c, w_ada, b_ada3) if more else None)
        x = _post_call(l, x, part, ga, ya, mod, n2w, *cast[:len(post_f32)])
        in_bf, mod = cast[len(post_f32):], mod_next
    return x
```

```python
import functools
import math

import numpy as np
import jax
import jax.numpy as jnp
from jax import lax
from jax.experimental import pallas as pl
from jax.experimental.pallas import tpu as pltpu

F32 = jnp.float32
BF16 = jnp.bfloat16

HEAD_DIM = 64
ATTN_HEADS = 8
ATTN_WIDTH = ATTN_HEADS * HEAD_DIM
ROPE_DIM = HEAD_DIM // 4
ROPE_HALF = ROPE_DIM // 2
ROPE_THETA = 500000.0
MOBA_BLOCK = 256
MOBA_TOPK = 3
POOL_WINDOWS = (2, 4, 8, 16)
CONV_K = 3
EPS = 1e-6
NEG = -1e30

LANES = 128
PAIR = LANES
HALO = 16
VMEM_LIMIT = 56 * 1024 * 1024

TM = 512
SUB = TM
FF_TN = 256
GATE_TN = 512
PV_BEHIND = 1
QK_AHEAD = 4
KV_GROUP = 2
BF16_SUBLANES = 16
V_ROWS = HEAD_DIM + BF16_SUBLANES
Q_SCALE = (1.0 / math.sqrt(HEAD_DIM)) * math.log2(math.e)


def _dot(a, b):
    return jnp.dot(a, b, preferred_element_type=F32)


def _const_spec(shape):
    nd = len(shape)
    return pl.BlockSpec(shape, lambda *_: (0,) * nd, pipeline_mode=pl.Buffered(1))


def _layer_spec(layer, shape):
    if len(shape) == 2:
        return _const_spec(shape)
    nd = len(shape)
    return pl.BlockSpec((None,) + tuple(shape[1:]), lambda *_: (layer,) + (0,) * (nd - 1),
                        pipeline_mode=pl.Buffered(1))


def _params(n_axes):
    return pltpu.CompilerParams(dimension_semantics=("arbitrary",) * n_axes,
                                vmem_limit_bytes=VMEM_LIMIT)


N_MOD = 6


def _mod_chunk(c_ref, w_ref, b_ref, o_ref):
    c = c_ref[...]
    c_act = (c * jax.nn.sigmoid(c)).astype(BF16)
    o_ref[...] = _dot(c_act, w_ref[...].astype(BF16)) + b_ref[...]


def _mod_specs(layer, n_batch, d_model, step_to_chunk):
    ins = [pl.BlockSpec((n_batch, d_model), lambda i: (0, 0)),
           pl.BlockSpec((None, d_model, d_model), lambda i: (layer, 0, step_to_chunk(i))),
           pl.BlockSpec((None, 1, d_model), lambda i: (layer, 0, step_to_chunk(i)))]
    out = pl.BlockSpec((None, n_batch, d_model), lambda i: (step_to_chunk(i), 0, 0))
    shape = jax.ShapeDtypeStruct((N_MOD, n_batch, d_model), F32)
    return ins, out, shape


def _modulation(layer, c, w_ada, b_ada3):
    B, D = c.shape
    ins, out, shape = _mod_specs(layer, B, D, lambda i: i)
    return pl.pallas_call(
        _mod_chunk, grid=(N_MOD,), in_specs=ins, out_specs=out, out_shape=shape,
        compiler_params=_params(1), name="adaln_modulation",
    )(c, w_ada, b_ada3)


def _rope_kernel(pos_ref, freq_ref, c_ref, s_ref):
    ang = freq_ref[...] * pos_ref[0].astype(F32)
    c_ref[0] = jnp.cos(ang)
    s_ref[0] = jnp.sin(ang)


def _rope_tables(positions):
    B, S = positions.shape
    freq = ROPE_THETA ** (-np.arange(0, ROPE_DIM, 2, dtype=np.float64) / ROPE_DIM)
    freq = jnp.asarray(freq.reshape(ROPE_HALF, 1), F32)
    spec = pl.BlockSpec((1, ROPE_HALF, S), lambda b: (b, 0, 0))
    shp = jax.ShapeDtypeStruct((B, ROPE_HALF, S), F32)
    return pl.pallas_call(
        _rope_kernel,
        grid=(B,),
        in_specs=[pl.BlockSpec((1, 1, S), lambda b: (b, 0, 0)),
                  pl.BlockSpec((ROPE_HALF, 1), lambda b: (0, 0))],
        out_specs=(spec, spec),
        out_shape=(shp, shp),
        compiler_params=_params(1),
        name="rotary_tables",
    )(positions.reshape(B, 1, S), freq)


def _in_kernel(x_ref, mod_ref, n1w_ref, win_ref, qw_ref, kw_ref,
               cos_ref, sin_ref, pwbd_ref, pscale_ref, convw_ref, ppool_ref, pconv_ref,
               part_ref, ga_ref, qT_ref, kx_ref, vx_ref, km_ref, ext_ref,
               *, d_model, pool_w, conv_w, attn_w):
    si = pl.program_id(1)
    n_sub = TM // SUB
    hs = {}

    def adaln(sub):
        x = x_ref[0, sub * SUB:(sub + 1) * SUB, :]
        ms = jnp.mean(x * x, axis=-1, keepdims=True)
        sh1 = mod_ref[0, pl.ds(pl.program_id(0), 1), :]
        sc1 = mod_ref[1, pl.ds(pl.program_id(0), 1), :]
        hs[sub] = ((x * lax.rsqrt(ms + EPS)) * (n1w_ref[...] * (1.0 + sc1)) + sh1).astype(BF16)

    @pl.when(si == 0)
    def _():
        ext_ref[0:HALO, :] = jnp.zeros((HALO, pool_w + conv_w), F32)
        km_ref[0] = jnp.zeros(km_ref.shape[1:], F32)

    o_pool = 0
    o_uc = o_pool + pool_w
    o_bc = o_uc + conv_w
    o_cc = o_bc + conv_w
    o_q = o_cc + conv_w
    o_k = o_q + attn_w
    o_v = o_k + attn_w
    o_g = o_v + attn_w
    n_pairs = attn_w // PAIR

    def sub_stages(sub):
        rows = slice(sub * SUB, (sub + 1) * SUB)
        blk0 = sub * (SUB // MOBA_BLOCK)
        cos_t = cos_ref[0, :, rows]
        sin_t = sin_ref[0, :, rows]

        def norm_rope_t(u, gain_ref):
            gain = gain_ref[...]
            outs = []
            for p in range(n_pairs):
                u_t = u[:, p * PAIR:(p + 1) * PAIR].T
                parts = []
                for hh in range(PAIR // HEAD_DIM):
                    xh = u_t[hh * HEAD_DIM:(hh + 1) * HEAD_DIM, :]
                    xn = xh * lax.rsqrt(jnp.mean(xh * xh, axis=0, keepdims=True) + EPS) * gain
                    x1 = xn[0:ROPE_HALF, :]
                    x2 = xn[ROPE_HALF:ROPE_DIM, :]
                    parts += [x1 * cos_t - x2 * sin_t, x2 * cos_t + x1 * sin_t, xn[ROPE_DIM:, :]]
                outs.append(jnp.concatenate(parts, axis=0))
            return outs

        def proj(lo, hi):
            return lambda: (_dot(hs[sub], win_ref[:, lo:hi]),)

        def q_stage(uq):
            for p, qp_t in enumerate(norm_rope_t(uq, qw_ref)):
                qT_ref[0, p * PAIR:(p + 1) * PAIR, rows] = (qp_t * Q_SCALE).astype(BF16)

        def k_stage(uk):
            row_blk = (si * (TM // MOBA_BLOCK) + blk0
                       + lax.broadcasted_iota(jnp.int32, (SUB, PAIR), 0) // MOBA_BLOCK)
            onehot = jnp.where(lax.broadcasted_iota(jnp.int32, (SUB, PAIR), 1) == row_blk, 1.0, 0.0).astype(BF16)
            for p, kp_t in enumerate(norm_rope_t(uk, kw_ref)):
                kp = kp_t.T
                kx_ref[0, rows, 2 * p * PAIR:(2 * p + 1) * PAIR] = kp.astype(BF16)
                kx_ref[0, rows, (2 * p + 1) * PAIR:(2 * p + 2) * PAIR] = onehot
                for j in range(SUB // MOBA_BLOCK):
                    lanes = slice(p * PAIR, (p + 1) * PAIR)
                    mean = jnp.mean(kp[j * MOBA_BLOCK:(j + 1) * MOBA_BLOCK, :], axis=0, keepdims=True)
                    blk_row = lax.broadcasted_iota(jnp.int32, (km_ref.shape[1], PAIR), 0)
                    mine = blk_row == si * (TM // MOBA_BLOCK) + blk0 + j
                    km_ref[0, :, lanes] = jnp.where(mine, mean, km_ref[0, :, lanes])

        def v_stage(uv):
            ones_rows = jnp.where(lax.broadcasted_iota(jnp.int32, (V_ROWS - HEAD_DIM, SUB), 0) == 0,
                                  1.0, 0.0).astype(BF16)
            for p in range(n_pairs):
                vT = uv[:, p * PAIR:(p + 1) * PAIR].T.astype(BF16)
                for hh in range(PAIR // HEAD_DIM):
                    r0 = (p * (PAIR // HEAD_DIM) + hh) * V_ROWS
                    vx_ref[0, r0:r0 + HEAD_DIM, rows] = vT[hh * HEAD_DIM:(hh + 1) * HEAD_DIM, :]
                    vx_ref[0, r0 + HEAD_DIM:r0 + V_ROWS, rows] = ones_rows

        mixed = {}

        def mixer_stage(ua):
            up = ua[:, o_pool:o_uc]
            uc = ua[:, o_uc:o_bc]
            bc = ua[:, o_bc:o_cc]
            cc = ua[:, o_cc:o_q]
            ext_ref[HALO:HALO + SUB, 0:pool_w] = up
            ext_ref[HALO:HALO + SUB, pool_w:pool_w + conv_w] = cc * uc
            ext = ext_ref[...]
            ext_ref[0:HALO, :] = ext[SUB:SUB + HALO, :]
            e_p = ext[:, 0:pool_w]
            e_c = ext[:, pool_w:pool_w + conv_w]

            s2 = e_p + pltpu.roll(e_p, 1, 0)
            s4 = s2 + pltpu.roll(s2, 2, 0)
            s8 = s4 + pltpu.roll(s4, 4, 0)
            s16 = s8 + pltpu.roll(s8, 8, 0)
            gdim = pool_w // len(POOL_WINDOWS)
            lane = lax.broadcasted_iota(jnp.int32, (1, pool_w), 1)
            wsum = jnp.where(lane < gdim, s2, jnp.where(lane < 2 * gdim, s4, jnp.where(lane < 3 * gdim, s8, s16)))
            win = jnp.where(lane < gdim, POOL_WINDOWS[0],
                            jnp.where(lane < 2 * gdim, POOL_WINDOWS[1],
                                      jnp.where(lane < 3 * gdim, POOL_WINDOWS[2], POOL_WINDOWS[3])))
            t = si * TM + sub * SUB + lax.broadcasted_iota(jnp.int32, (SUB, pool_w), 0)
            cnt = jnp.minimum(t + 1, win).astype(F32)
            pooled = wsum[HALO:, :] / cnt - up
            yp = _dot(pooled.astype(BF16), pwbd_ref[...]) * pscale_ref[...]
            mixed["pool"] = _dot(yp.astype(BF16), ppool_ref[...])

            cw = convw_ref[...]
            conv = cw[0:1, :] * pltpu.roll(e_c, 2, 0) + cw[1:2, :] * pltpu.roll(e_c, 1, 0) + cw[2:3, :] * e_c
            yc = bc * conv[HALO:, :]
            mixed["conv"] = _dot(yc.astype(BF16), pconv_ref[...])

        def gate_dots(c):
            cols = [slice(o_g + i * d_model + c * GATE_TN, o_g + i * d_model + (c + 1) * GATE_TN) for i in range(3)]
            return lambda: tuple(_dot(hs[sub], win_ref[:, cs]) for cs in cols)

        def gate_stage(c):
            out = slice(c * GATE_TN, (c + 1) * GATE_TN)

            def run(g0, g1, g2):
                merged = jax.nn.sigmoid(g0) * mixed["pool"][:, out] + jax.nn.sigmoid(g1) * mixed["conv"][:, out]
                part_ref[0, rows, out] = merged.astype(BF16)
                ga_ref[0, rows, out] = jax.nn.sigmoid(g2).astype(BF16)
            return run

        stages = [(proj(o_q, o_k), q_stage), (proj(o_k, o_v), k_stage)]
        if sub + 1 < n_sub:
            stages.append((lambda: (), lambda: adaln(sub + 1)))
        stages += [(proj(o_v, o_g), v_stage), (proj(o_pool, o_q), mixer_stage)]
        stages += [(gate_dots(c), gate_stage(c)) for c in range(d_model // GATE_TN)]
        return stages

    adaln(0)
    stages = [st for sub in range(n_sub) for st in sub_stages(sub)]
    res = stages[0][0]()
    for i, (_, finish) in enumerate(stages):
        nxt = stages[i + 1][0]() if i + 1 < len(stages) else None
        finish(*res)
        res = nxt


def _in_call(layer, x, mod, n1w, w_in, qw, kw, cos_t, sin_t, pwbd, pscale, convw, ppool, pconv):
    B, S, D = x.shape
    pool_w = pwbd.shape[-1]
    conv_w = convw.shape[-1]
    A = ATTN_WIDTH
    nt = S // TM
    row = lambda b, s: (b, s, 0)
    tr = lambda b, s: (b, 0, s)
    lspec = functools.partial(_layer_spec, layer)
    kern = functools.partial(_in_kernel, d_model=D, pool_w=pool_w, conv_w=conv_w, attn_w=A)
    return pl.pallas_call(
        kern,
        grid=(B, nt),
        in_specs=[pl.BlockSpec((1, TM, D), row),
                  _const_spec(mod.shape),
                  lspec(n1w.shape), lspec(w_in.shape),
                  lspec(qw.shape), lspec(kw.shape),
                  pl.BlockSpec((1, ROPE_HALF, TM), tr), pl.BlockSpec((1, ROPE_HALF, TM), tr),
                  lspec(pwbd.shape), lspec(pscale.shape), lspec(convw.shape),
                  lspec(ppool.shape), lspec(pconv.shape)],
        out_specs=(pl.BlockSpec((1, TM, D), row), pl.BlockSpec((1, TM, D), row),
                   pl.BlockSpec((1, A, TM), tr), pl.BlockSpec((1, TM, 2 * A), row),
                   pl.BlockSpec((1, ATTN_HEADS * V_ROWS, TM), tr),
                   pl.BlockSpec((1, S // MOBA_BLOCK, A), lambda b, s: (b, 0, 0))),
        out_shape=(jax.ShapeDtypeStruct((B, S, D), BF16), jax.ShapeDtypeStruct((B, S, D), BF16),
                   jax.ShapeDtypeStruct((B, A, S), BF16), jax.ShapeDtypeStruct((B, S, 2 * A), BF16),
                   jax.ShapeDtypeStruct((B, ATTN_HEADS * V_ROWS, S), BF16),
                   jax.ShapeDtypeStruct((B, S // MOBA_BLOCK, A), F32)),
        scratch_shapes=[pltpu.VMEM((SUB + HALO, pool_w + conv_w), F32)],
        compiler_params=_params(2),
        name="in_proj_mixers",
    )(x, mod, n1w, w_in, qw, kw, cos_t, sin_t, pwbd, pscale, convw, ppool, pconv)


def _attn_kernel(qT_ref, kx_ref, vx_ref, km_ref, *refs, n_blocks, n_cast, with_mod):
    n_in = n_cast + (3 if with_mod else 0)
    o_ref = refs[n_in]
    for w_ref, wb_ref in zip(refs[:n_cast], refs[n_in + 1:n_in + 1 + n_cast]):
        wb_ref[...] = w_ref[...].astype(BF16)
    if with_mod:
        _mod_chunk(*refs[n_cast:n_in], refs[n_in + 1 + n_cast])

    TQ = MOBA_BLOCK
    heads_per_pair = PAIR // HEAD_DIM
    km = km_ref[0]
    blk = lax.broadcasted_iota(jnp.int32, (n_blocks, TQ), 0)
    half = lax.broadcasted_iota(jnp.int32, (PAIR, TQ), 0) // HEAD_DIM
    bias_pad = jnp.zeros((BF16_SUBLANES - n_blocks, TQ), F32)
    rhs_pad = jnp.zeros((PAIR - BF16_SUBLANES, TQ), BF16)

    def make_rhs(qi, h):
        p, hh = divmod(h, heads_per_pair)
        lanes = slice(p * PAIR, (p + 1) * PAIR)
        qT_pair = qT_ref[0, lanes, qi * TQ:(qi + 1) * TQ]
        qT_h = jnp.where(half == hh, qT_pair, jnp.zeros_like(qT_pair))
        if qi <= MOBA_TOPK:
            sel = blk <= qi
        else:
            past = blk < qi
            km_pair = km[:, lanes]
            km_hi = km_pair.astype(BF16)
            km_lo = (km_pair - km_hi.astype(F32)).astype(BF16)
            gate = _dot(km_hi, qT_h) + _dot(km_lo, qT_h)
            gate = jnp.where(past, gate, NEG)
            beaten = jnp.zeros((n_blocks, TQ), jnp.int32)
            for m in range(n_blocks):
                gm = gate[m:m + 1, :]
                ahead = (gm > gate) | ((gm == gate) & (m < blk))
                beaten = beaten + ahead.astype(jnp.int32)
            sel = (past & (beaten < MOBA_TOPK)) | (blk == qi)
        bias = jnp.concatenate([jnp.where(sel, 0.0, NEG), bias_pad], axis=0).astype(BF16)
        return jnp.concatenate([qT_h, bias, rhs_pad], axis=0)

    items = []
    for qi in range(n_blocks):
        hi = qi
        while hi >= 0:
            lo = max(hi - KV_GROUP + 1, 0)
            items += [(qi, lo, hi, h) for h in range(ATTN_HEADS)]
            hi = lo - 1
    rhs = {}
    run_max = {}
    run_acc = {}
    done = {}

    def causal_mask(n_keys):
        kpos = lax.broadcasted_iota(jnp.int32, (n_keys, TQ), 0)
        qpos = lax.broadcasted_iota(jnp.int32, (n_keys, TQ), 1)
        return kpos <= qpos + (n_keys - TQ)

    def scores(item):
        qi, lo, hi, h = item
        if (qi, h) not in rhs:
            rhs[(qi, h)] = make_rhs(qi, h)
        p = h // heads_per_pair
        s = _dot(kx_ref[0, lo * TQ:(hi + 1) * TQ, 2 * p * PAIR:(2 * p + 2) * PAIR], rhs[(qi, h)])
        return s.astype(BF16)

    def values(qi, lo, hi, h, pr, alpha):
        acc = _dot(vx_ref[0, h * V_ROWS:(h + 1) * V_ROWS, lo * TQ:(hi + 1) * TQ], pr)
        if alpha is not None:
            acc = alpha * run_acc[(qi, h)] + acc
        run_acc[(qi, h)] = acc
        if lo == 0:
            done[h] = acc[0:HEAD_DIM, :] / acc[HEAD_DIM:HEAD_DIM + 1, :]
            del run_acc[(qi, h)], run_max[(qi, h)], rhs[(qi, h)]
            if h == ATTN_HEADS - 1:
                oT = jnp.concatenate([done[j] for j in range(ATTN_HEADS)], axis=0)
                o_ref[0, qi * TQ:(qi + 1) * TQ, :] = oT.T.astype(BF16)

    pending = [scores(it) for it in items[:QK_AHEAD]]
    late = []
    for i, (qi, lo, hi, h) in enumerate(items):
        if i + QK_AHEAD < len(items):
            pending.append(scores(items[i + QK_AHEAD]))
        s = pending.pop(0)
        if hi == qi:
            s = jnp.where(causal_mask((hi - lo + 1) * TQ), s, jnp.asarray(NEG, BF16))
        m_n = jnp.max(s, axis=0, keepdims=True).astype(F32)
        alpha = None
        if (qi, h) in run_max:
            m_n = jnp.maximum(run_max[(qi, h)], m_n)
            alpha = jnp.exp2(run_max[(qi, h)] - m_n)
        run_max[(qi, h)] = m_n
        late.append((qi, lo, hi, h, jnp.exp2(s - m_n.astype(BF16)), alpha))
        if len(late) > PV_BEHIND:
            values(*late.pop(0))
    for job in late:
        values(*job)


def _attn_call(qT, kx, vx, km, casts, mod_job):
    B, A, S = qT.shape
    nb = S // MOBA_BLOCK
    assert nb <= BF16_SUBLANES and nb <= PAIR
    kern = functools.partial(_attn_kernel, n_blocks=nb, n_cast=len(casts), with_mod=mod_job is not None)
    whole = lambda b: (b, 0, 0)
    mod_in, mod_out, mod_shape, mod_args = [], [], [], []
    if mod_job is not None:
        layer, c, w_ada, b_ada3 = mod_job
        assert B >= N_MOD
        mod_in, out, shape = _mod_specs(layer, B, c.shape[1], lambda i: jnp.minimum(i, N_MOD - 1))
        mod_out, mod_shape, mod_args = [out], [shape], [c, w_ada, b_ada3]
    cast_in, cast_out, cast_shape = [], [], []
    for w, layer in casts:
        _, rows, cols = w.shape
        assert rows % (B * BF16_SUBLANES) == 0
        cast_in.append(pl.BlockSpec((None, rows // B, cols), lambda b, layer=layer: (layer, b, 0)))
        cast_out.append(pl.BlockSpec((rows // B, cols), lambda b: (b, 0)))
        cast_shape.append(jax.ShapeDtypeStruct((rows, cols), BF16))
    out = pl.pallas_call(
        kern,
        grid=(B,),
        in_specs=[pl.BlockSpec((1, A, S), whole),
                  pl.BlockSpec((1, S, 2 * A), whole),
                  pl.BlockSpec((1, ATTN_HEADS * V_ROWS, S), whole),
                  pl.BlockSpec((1, nb, A), whole)] + cast_in + mod_in,
        out_specs=[pl.BlockSpec((1, S, A), whole)] + cast_out + mod_out,
        out_shape=[jax.ShapeDtypeStruct((B, S, A), BF16)] + cast_shape + mod_shape,
        compiler_params=_params(1),
        name="moba_attention",
    )(qT, kx, vx, km, *[w for w, _ in casts], *mod_args)
    n_cast = len(casts)
    return out[0], out[1:1 + n_cast], (out[1 + n_cast] if mod_job is not None else None)


def _post_kernel(x_ref, part_ref, ga_ref, ya_ref, mod_ref, n2w_ref,
                 pattn_ref, wout_ref, wg_ref, wu_ref, wd_ref, o_ref, act_ref):
    row = pl.ds(pl.program_id(0), 1)
    g1 = mod_ref[2, row, :]
    sh2 = mod_ref[3, row, :]
    sc2 = mod_ref[4, row, :]
    g2 = mod_ref[5, row, :]
    merged = part_ref[0].astype(F32) + ga_ref[0].astype(F32) * _dot(ya_ref[0], pattn_ref[...])
    x1 = x_ref[0] + g1 * _dot(merged.astype(BF16), wout_ref[...])
    ms = jnp.mean(x1 * x1, axis=-1, keepdims=True)
    h2 = (x1 * lax.rsqrt(ms + EPS)) * (n2w_ref[...] * (1.0 + sc2)) + sh2
    h2 = h2.astype(BF16)
    d_ff = wg_ref.shape[1]

    def ff_dots(j):
        cols = slice(j * FF_TN, (j + 1) * FF_TN)
        return _dot(h2, wg_ref[:, cols]), _dot(h2, wu_ref[:, cols])

    res = ff_dots(0)
    for j in range(d_ff // FF_TN):
        nxt = ff_dots(j + 1) if (j + 1) * FF_TN < d_ff else None
        g, u = res
        act_ref[:, j * FF_TN:(j + 1) * FF_TN] = (g * jax.nn.sigmoid(g) * u).astype(BF16)
        res = nxt
    o_ref[0] = x1 + g2 * _dot(act_ref[...], wd_ref[...])


def _post_call(layer, x, part, ga, ya, mod, n2w, pattn, wout, wg, wu, wd):
    B, S, D = x.shape
    A = ya.shape[-1]
    d_ff = wg.shape[-1]
    assert d_ff % FF_TN == 0
    row = lambda b, s: (b, s, 0)
    lspec = functools.partial(_layer_spec, layer)
    return pl.pallas_call(
        _post_kernel,
        grid=(B, S // TM),
        in_specs=[pl.BlockSpec((1, TM, D), row), pl.BlockSpec((1, TM, D), row),
                  pl.BlockSpec((1, TM, D), row), pl.BlockSpec((1, TM, A), row),
                  _const_spec(mod.shape),
                  lspec(n2w.shape),
                  lspec(pattn.shape), lspec(wout.shape), lspec(wg.shape), lspec(wu.shape), lspec(wd.shape)],
        out_specs=pl.BlockSpec((1, TM, D), row),
        out_shape=jax.ShapeDtypeStruct((B, S, D), F32),
        scratch_shapes=[pltpu.VMEM((TM, d_ff), BF16)],
        compiler_params=_params(2),
        name="merge_swiglu",
    )(x, part, ga, ya, mod, n2w, pattn, wout, wg, wu, wd)


def kernel(x, c, positions, norm1_w, norm2_w, w_ada, b_ada, w_in, pool_w, pool_scale, conv_w,
           q_norm_w, k_norm_w, p_pool, p_conv, p_attn, w_out, w_gate, w_up, w_down):
    B, S, D = x.shape
    L = w_ada.shape[0]
    assert S % TM == 0 and TM % SUB == 0 and SUB % MOBA_BLOCK == 0 and HALO >= max(POOL_WINDOWS) - 1
    n_groups, gdim = pool_w.shape[1], pool_w.shape[2]
    assert n_groups == len(POOL_WINDOWS) and conv_w.shape[1] == CONV_K

    assert w_ada.shape[2] == N_MOD * D
    b_ada3 = b_ada.reshape(L, 1, N_MOD * D)
    mod = _modulation(0, c, w_ada, b_ada3)
    cos_t, sin_t = _rope_tables(positions)

    pwbd = jnp.zeros((L, n_groups * gdim, n_groups * gdim), F32)
    for g in range(n_groups):
        pwbd = pwbd.at[:, g * gdim:(g + 1) * gdim, g * gdim:(g + 1) * gdim].set(pool_w[:, g])
    pwbd = pwbd.astype(BF16)
    qw = q_norm_w.reshape(L, HEAD_DIM, 1)
    kw = k_norm_w.reshape(L, HEAD_DIM, 1)
    n1w = norm1_w.reshape(L, 1, D)
    n2w = norm2_w.reshape(L, 1, D)
    pscale = pool_scale.reshape(L, 1, -1)

    in_f32 = (w_in, p_pool, p_conv)
    post_f32 = (p_attn, w_out, w_gate, w_up, w_down)
    in_bf = [w[0].astype(BF16) for w in in_f32]

    for l in range(L):
        part, ga, qT, kx, vx, km = _in_call(l, x, mod, n1w, in_bf[0], qw, kw, cos_t, sin_t, pwbd,
                                            pscale, conv_w, in_bf[1], in_bf[2])
        more = l + 1 < L
        casts = [(w, l) for w in post_f32] + ([(w, l + 1) for w in in_f32] if more else [])
        ya, cast, mod_next = _attn_call(qT, kx, vx, km, casts, (l + 1, c, w_ada, b_ada3) if more else None)
        x = _post_call(l, x, part, ga, ya, mod, n2w, *cast[:len(post_f32)])
        in_bf, mod = cast[len(post_f32):], mod_next
    return x
```

```python
import functools
import math

import numpy as np
import jax
import jax.numpy as jnp
from jax import lax
from jax.experimental import pallas as pl
from jax.experimental.pallas import tpu as pltpu

F32 = jnp.float32
BF16 = jnp.bfloat16

HEAD_DIM = 64
ATTN_HEADS = 8
ATTN_WIDTH = ATTN_HEADS * HEAD_DIM
ROPE_DIM = HEAD_DIM // 4
ROPE_HALF = ROPE_DIM // 2
ROPE_THETA = 500000.0
MOBA_BLOCK = 256
MOBA_TOPK = 3
POOL_WINDOWS = (2, 4, 8, 16)
CONV_K = 3
EPS = 1e-6
NEG = -1e30

LANES = 128
PAIR = LANES
HALO = 16
VMEM_LIMIT = 56 * 1024 * 1024

TM = 512
SUB = TM
FF_TN = 256
GATE_TN = 512
PV_BEHIND = 1
QK_AHEAD = 4
KV_GROUP = 2
BF16_SUBLANES = 16
V_ROWS = HEAD_DIM + BF16_SUBLANES
Q_SCALE = (1.0 / math.sqrt(HEAD_DIM)) * math.log2(math.e)


def _dot(a, b):
    return jnp.dot(a, b, preferred_element_type=F32)


def _const_spec(shape):
    nd = len(shape)
    return pl.BlockSpec(shape, lambda *_: (0,) * nd, pipeline_mode=pl.Buffered(1))


def _layer_spec(layer, shape):
    if len(shape) == 2:
        return _const_spec(shape)
    nd = len(shape)
    return pl.BlockSpec((None,) + tuple(shape[1:]), lambda *_: (layer,) + (0,) * (nd - 1),
                        pipeline_mode=pl.Buffered(1))


def _params(n_axes):
    return pltpu.CompilerParams(dimension_semantics=("arbitrary",) * n_axes,
                                vmem_limit_bytes=VMEM_LIMIT)


N_MOD = 6


def _mod_chunk(c_ref, w_ref, b_ref, o_ref):
    c = c_ref[...]
    c_act = (c * jax.nn.sigmoid(c)).astype(BF16)
    o_ref[...] = _dot(c_act, w_ref[...].astype(BF16)) + b_ref[...]


def _mod_specs(layer, n_batch, d_model, step_to_chunk):
    ins = [pl.BlockSpec((n_batch, d_model), lambda i: (0, 0)),
           pl.BlockSpec((None, d_model, d_model), lambda i: (layer, 0, step_to_chunk(i))),
           pl.BlockSpec((None, 1, d_model), lambda i: (layer, 0, step_to_chunk(i)))]
    out = pl.BlockSpec((None, n_batch, d_model), lambda i: (step_to_chunk(i), 0, 0))
    shape = jax.ShapeDtypeStruct((N_MOD, n_batch, d_model), F32)
    return ins, out, shape


def _modulation(layer, c, w_ada, b_ada3):
    B, D = c.shape
    ins, out, shape = _mod_specs(layer, B, D, lambda i: i)
    return pl.pallas_call(
        _mod_chunk, grid=(N_MOD,), in_specs=ins, out_specs=out, out_shape=shape,
        compiler_params=_params(1), name="adaln_modulation",
    )(c, w_ada, b_ada3)


def _rope_kernel(pos_ref, freq_ref, c_ref, s_ref):
    ang = freq_ref[...] * pos_ref[0].astype(F32)
    c_ref[0] = jnp.cos(ang)
    s_ref[0] = jnp.sin(ang)


def _rope_tables(positions):
    B, S = positions.shape
    freq = ROPE_THETA ** (-np.arange(0, ROPE_DIM, 2, dtype=np.float64) / ROPE_DIM)
    freq = jnp.asarray(freq.reshape(ROPE_HALF, 1), F32)
    spec = pl.BlockSpec((1, ROPE_HALF, S), lambda b: (b, 0, 0))
    shp = jax.ShapeDtypeStruct((B, ROPE_HALF, S), F32)
    return pl.pallas_call(
        _rope_kernel,
        grid=(B,),
        in_specs=[pl.BlockSpec((1, 1, S), lambda b: (b, 0, 0)),
                  pl.BlockSpec((ROPE_HALF, 1), lambda b: (0, 0))],
        out_specs=(spec, spec),
        out_shape=(shp, shp),
        compiler_params=_params(1),
        name="rotary_tables",
    )(positions.reshape(B, 1, S), freq)


def _in_kernel(x_ref, mod_ref, n1w_ref, win_ref, qw_ref, kw_ref,
               cos_ref, sin_ref, pwbd_ref, pscale_ref, convw_ref, ppool_ref, pconv_ref,
               part_ref, ga_ref, qT_ref, kx_ref, vx_ref, km_ref, ext_ref,
               *, layer, d_model, pool_w, conv_w, attn_w):
    si = pl.program_id(1)
    n_sub = TM // SUB
    hs = {}

    def adaln(sub):
        x = x_ref[0, sub * SUB:(sub + 1) * SUB, :]
        ms = jnp.mean(x * x, axis=-1, keepdims=True)
        sh1 = mod_ref[0, pl.ds(pl.program_id(0), 1), :]
        sc1 = mod_ref[1, pl.ds(pl.program_id(0), 1), :]
        gain = n1w_ref[layer:layer + 1, :]
        hs[sub] = ((x * lax.rsqrt(ms + EPS)) * (gain * (1.0 + sc1)) + sh1).astype(BF16)

    @pl.when(si == 0)
    def _():
        ext_ref[0:HALO, :] = jnp.zeros((HALO, pool_w + conv_w), F32)
        km_ref[0] = jnp.zeros(km_ref.shape[1:], F32)

    o_pool = 0
    o_uc = o_pool + pool_w
    o_bc = o_uc + conv_w
    o_cc = o_bc + conv_w
    o_q = o_cc + conv_w
    o_k = o_q + attn_w
    o_v = o_k + attn_w
    o_g = o_v + attn_w
    n_pairs = attn_w // PAIR

    def sub_stages(sub):
        rows = slice(sub * SUB, (sub + 1) * SUB)
        blk0 = sub * (SUB // MOBA_BLOCK)
        cos_t = cos_ref[0, :, rows]
        sin_t = sin_ref[0, :, rows]

        def norm_rope_t(u, gain_ref):
            gain = gain_ref[...]
            outs = []
            for p in range(n_pairs):
                u_t = u[:, p * PAIR:(p + 1) * PAIR].T
                parts = []
                for hh in range(PAIR // HEAD_DIM):
                    xh = u_t[hh * HEAD_DIM:(hh + 1) * HEAD_DIM, :]
                    xn = xh * lax.rsqrt(jnp.mean(xh * xh, axis=0, keepdims=True) + EPS) * gain
                    x1 = xn[0:ROPE_HALF, :]
                    x2 = xn[ROPE_HALF:ROPE_DIM, :]
                    parts += [x1 * cos_t - x2 * sin_t, x2 * cos_t + x1 * sin_t, xn[ROPE_DIM:, :]]
                outs.append(jnp.concatenate(parts, axis=0))
            return outs

        def proj(lo, hi):
            return lambda: (_dot(hs[sub], win_ref[:, lo:hi]),)

        def q_stage(uq):
            for p, qp_t in enumerate(norm_rope_t(uq, qw_ref)):
                qT_ref[0, p * PAIR:(p + 1) * PAIR, rows] = (qp_t * Q_SCALE).astype(BF16)

        def k_stage(uk):
            row_blk = (si * (TM // MOBA_BLOCK) + blk0
                       + lax.broadcasted_iota(jnp.int32, (SUB, PAIR), 0) // MOBA_BLOCK)
            onehot = jnp.where(lax.broadcasted_iota(jnp.int32, (SUB, PAIR), 1) == row_blk, 1.0, 0.0).astype(BF16)
            for p, kp_t in enumerate(norm_rope_t(uk, kw_ref)):
                kp = kp_t.T
                kx_ref[0, rows, 2 * p * PAIR:(2 * p + 1) * PAIR] = kp.astype(BF16)
                kx_ref[0, rows, (2 * p + 1) * PAIR:(2 * p + 2) * PAIR] = onehot
                for j in range(SUB // MOBA_BLOCK):
                    lanes = slice(p * PAIR, (p + 1) * PAIR)
                    mean = jnp.mean(kp[j * MOBA_BLOCK:(j + 1) * MOBA_BLOCK, :], axis=0, keepdims=True)
                    blk_row = lax.broadcasted_iota(jnp.int32, (km_ref.shape[1], PAIR), 0)
                    mine = blk_row == si * (TM // MOBA_BLOCK) + blk0 + j
                    km_ref[0, :, lanes] = jnp.where(mine, mean, km_ref[0, :, lanes])

        def v_stage(uv):
            ones_rows = jnp.where(lax.broadcasted_iota(jnp.int32, (V_ROWS - HEAD_DIM, SUB), 0) == 0,
                                  1.0, 0.0).astype(BF16)
            for p in range(n_pairs):
                vT = uv[:, p * PAIR:(p + 1) * PAIR].T.astype(BF16)
                for hh in range(PAIR // HEAD_DIM):
                    r0 = (p * (PAIR // HEAD_DIM) + hh) * V_ROWS
                    vx_ref[0, r0:r0 + HEAD_DIM, rows] = vT[hh * HEAD_DIM:(hh + 1) * HEAD_DIM, :]
                    vx_ref[0, r0 + HEAD_DIM:r0 + V_ROWS, rows] = ones_rows

        mixed = {}

        def mixer_stage(ua):
            up = ua[:, o_pool:o_uc]
            uc = ua[:, o_uc:o_bc]
            bc = ua[:, o_bc:o_cc]
            cc = ua[:, o_cc:o_q]
            ext_ref[HALO:HALO + SUB, 0:pool_w] = up
            ext_ref[HALO:HALO + SUB, pool_w:pool_w + conv_w] = cc * uc
            ext = ext_ref[...]
            ext_ref[0:HALO, :] = ext[SUB:SUB + HALO, :]
            e_p = ext[:, 0:pool_w]
            e_c = ext[:, pool_w:pool_w + conv_w]

            s2 = e_p + pltpu.roll(e_p, 1, 0)
            s4 = s2 + pltpu.roll(s2, 2, 0)
            s8 = s4 + pltpu.roll(s4, 4, 0)
            s16 = s8 + pltpu.roll(s8, 8, 0)
            gdim = pool_w // len(POOL_WINDOWS)
            lane = lax.broadcasted_iota(jnp.int32, (1, pool_w), 1)
            wsum = jnp.where(lane < gdim, s2, jnp.where(lane < 2 * gdim, s4, jnp.where(lane < 3 * gdim, s8, s16)))
            win = jnp.where(lane < gdim, POOL_WINDOWS[0],
                            jnp.where(lane < 2 * gdim, POOL_WINDOWS[1],
                                      jnp.where(lane < 3 * gdim, POOL_WINDOWS[2], POOL_WINDOWS[3])))
            t = si * TM + sub * SUB + lax.broadcasted_iota(jnp.int32, (SUB, pool_w), 0)
            cnt = jnp.minimum(t + 1, win).astype(F32)
            pooled = wsum[HALO:, :] / cnt - up
            yp = _dot(pooled.astype(BF16), pwbd_ref[...]) * pscale_ref[layer:layer + 1, :]
            mixed["pool"] = _dot(yp.astype(BF16), ppool_ref[...])

            cw = convw_ref[...]
            conv = cw[0:1, :] * pltpu.roll(e_c, 2, 0) + cw[1:2, :] * pltpu.roll(e_c, 1, 0) + cw[2:3, :] * e_c
            yc = bc * conv[HALO:, :]
            mixed["conv"] = _dot(yc.astype(BF16), pconv_ref[...])

        def gate_dots(c):
            cols = [slice(o_g + i * d_model + c * GATE_TN, o_g + i * d_model + (c + 1) * GATE_TN) for i in range(3)]
            return lambda: tuple(_dot(hs[sub], win_ref[:, cs]) for cs in cols)

        def gate_stage(c):
            out = slice(c * GATE_TN, (c + 1) * GATE_TN)

            def run(g0, g1, g2):
                merged = jax.nn.sigmoid(g0) * mixed["pool"][:, out] + jax.nn.sigmoid(g1) * mixed["conv"][:, out]
                part_ref[0, rows, out] = merged.astype(BF16)
                ga_ref[0, rows, out] = jax.nn.sigmoid(g2).astype(BF16)
            return run

        stages = [(proj(o_q, o_k), q_stage), (proj(o_k, o_v), k_stage)]
        if sub + 1 < n_sub:
            stages.append((lambda: (), lambda: adaln(sub + 1)))
        stages += [(proj(o_v, o_g), v_stage), (proj(o_pool, o_q), mixer_stage)]
        stages += [(gate_dots(c), gate_stage(c)) for c in range(d_model // GATE_TN)]
        return stages

    adaln(0)
    stages = [st for sub in range(n_sub) for st in sub_stages(sub)]
    res = stages[0][0]()
    for i, (_, finish) in enumerate(stages):
        nxt = stages[i + 1][0]() if i + 1 < len(stages) else None
        finish(*res)
        res = nxt


def _in_call(layer, x, mod, n1w, w_in, qw, kw, cos_t, sin_t, pwbd, pscale, convw, ppool, pconv):
    B, S, D = x.shape
    pool_w = pwbd.shape[-1]
    conv_w = convw.shape[-1]
    A = ATTN_WIDTH
    nt = S // TM
    row = lambda b, s: (b, s, 0)
    tr = lambda b, s: (b, 0, s)
    lspec = functools.partial(_layer_spec, layer)
    kern = functools.partial(_in_kernel, layer=layer, d_model=D, pool_w=pool_w, conv_w=conv_w, attn_w=A)
    return pl.pallas_call(
        kern,
        grid=(B, nt),
        in_specs=[pl.BlockSpec((1, TM, D), row),
                  _const_spec(mod.shape),
                  lspec(n1w.shape), lspec(w_in.shape),
                  lspec(qw.shape), lspec(kw.shape),
                  pl.BlockSpec((1, ROPE_HALF, TM), tr), pl.BlockSpec((1, ROPE_HALF, TM), tr),
                  lspec(pwbd.shape), lspec(pscale.shape), lspec(convw.shape),
                  lspec(ppool.shape), lspec(pconv.shape)],
        out_specs=(pl.BlockSpec((1, TM, D), row), pl.BlockSpec((1, TM, D), row),
                   pl.BlockSpec((1, A, TM), tr), pl.BlockSpec((1, TM, 2 * A), row),
                   pl.BlockSpec((1, ATTN_HEADS * V_ROWS, TM), tr),
                   pl.BlockSpec((1, S // MOBA_BLOCK, A), lambda b, s: (b, 0, 0))),
        out_shape=(jax.ShapeDtypeStruct((B, S, D), BF16), jax.ShapeDtypeStruct((B, S, D), BF16),
                   jax.ShapeDtypeStruct((B, A, S), BF16), jax.ShapeDtypeStruct((B, S, 2 * A), BF16),
                   jax.ShapeDtypeStruct((B, ATTN_HEADS * V_ROWS, S), BF16),
                   jax.ShapeDtypeStruct((B, S // MOBA_BLOCK, A), F32)),
        scratch_shapes=[pltpu.VMEM((SUB + HALO, pool_w + conv_w), F32)],
        compiler_params=_params(2),
        name="in_proj_mixers",
    )(x, mod, n1w, w_in, qw, kw, cos_t, sin_t, pwbd, pscale, convw, ppool, pconv)


def _attn_kernel(qT_ref, kx_ref, vx_ref, km_ref, *refs, n_blocks, n_cast, with_mod):
    n_in = n_cast + (3 if with_mod else 0)
    o_ref = refs[n_in]
    for w_ref, wb_ref in zip(refs[:n_cast], refs[n_in + 1:n_in + 1 + n_cast]):
        wb_ref[...] = w_ref[...].astype(BF16)
    if with_mod:
        _mod_chunk(*refs[n_cast:n_in], refs[n_in + 1 + n_cast])

    TQ = MOBA_BLOCK
    heads_per_pair = PAIR // HEAD_DIM
    km = km_ref[0]
    blk = lax.broadcasted_iota(jnp.int32, (n_blocks, TQ), 0)
    half = lax.broadcasted_iota(jnp.int32, (PAIR, TQ), 0) // HEAD_DIM
    bias_pad = jnp.zeros((BF16_SUBLANES - n_blocks, TQ), F32)
    rhs_pad = jnp.zeros((PAIR - BF16_SUBLANES, TQ), BF16)

    def make_rhs(qi, h):
        p, hh = divmod(h, heads_per_pair)
        lanes = slice(p * PAIR, (p + 1) * PAIR)
        qT_pair = qT_ref[0, lanes, qi * TQ:(qi + 1) * TQ]
        qT_h = jnp.where(half == hh, qT_pair, jnp.zeros_like(qT_pair))
        if qi <= MOBA_TOPK:
            sel = blk <= qi
        else:
            past = blk < qi
            km_pair = km[:, lanes]
            km_hi = km_pair.astype(BF16)
            km_lo = (km_pair - km_hi.astype(F32)).astype(BF16)
            gate = _dot(km_hi, qT_h) + _dot(km_lo, qT_h)
            gate = jnp.where(past, gate, NEG)
            beaten = jnp.zeros((n_blocks, TQ), jnp.int32)
            for m in range(n_blocks):
                gm = gate[m:m + 1, :]
                ahead = (gm > gate) | ((gm == gate) & (m < blk))
                beaten = beaten + ahead.astype(jnp.int32)
            sel = (past & (beaten < MOBA_TOPK)) | (blk == qi)
        bias = jnp.concatenate([jnp.where(sel, 0.0, NEG), bias_pad], axis=0).astype(BF16)
        return jnp.concatenate([qT_h, bias, rhs_pad], axis=0)

    items = []
    for qi in range(n_blocks):
        hi = qi
        while hi >= 0:
            lo = max(hi - KV_GROUP + 1, 0)
            items += [(qi, lo, hi, h) for h in range(ATTN_HEADS)]
            hi = lo - 1
    rhs = {}
    run_max = {}
    run_acc = {}
    done = {}

    def causal_mask(n_keys):
        kpos = lax.broadcasted_iota(jnp.int32, (n_keys, TQ), 0)
        qpos = lax.broadcasted_iota(jnp.int32, (n_keys, TQ), 1)
        return kpos <= qpos + (n_keys - TQ)

    def scores(item):
        qi, lo, hi, h = item
        if (qi, h) not in rhs:
            rhs[(qi, h)] = make_rhs(qi, h)
        p = h // heads_per_pair
        s = _dot(kx_ref[0, lo * TQ:(hi + 1) * TQ, 2 * p * PAIR:(2 * p + 2) * PAIR], rhs[(qi, h)])
        return s.astype(BF16)

    def values(qi, lo, hi, h, pr, alpha):
        acc = _dot(vx_ref[0, h * V_ROWS:(h + 1) * V_ROWS, lo * TQ:(hi + 1) * TQ], pr)
        if alpha is not None:
            acc = alpha * run_acc[(qi, h)] + acc
        run_acc[(qi, h)] = acc
        if lo == 0:
            done[h] = acc[0:HEAD_DIM, :] / acc[HEAD_DIM:HEAD_DIM + 1, :]
            del run_acc[(qi, h)], run_max[(qi, h)], rhs[(qi, h)]
            if h == ATTN_HEADS - 1:
                oT = jnp.concatenate([done[j] for j in range(ATTN_HEADS)], axis=0)
                o_ref[0, qi * TQ:(qi + 1) * TQ, :] = oT.T.astype(BF16)

    pending = [scores(it) for it in items[:QK_AHEAD]]
    late = []
    for i, (qi, lo, hi, h) in enumerate(items):
        if i + QK_AHEAD < len(items):
            pending.append(scores(items[i + QK_AHEAD]))
        s = pending.pop(0)
        if hi == qi:
            s = jnp.where(causal_mask((hi - lo + 1) * TQ), s, jnp.asarray(NEG, BF16))
        m_n = jnp.max(s, axis=0, keepdims=True).astype(F32)
        alpha = None
        if (qi, h) in run_max:
            m_n = jnp.maximum(run_max[(qi, h)], m_n)
            alpha = jnp.exp2(run_max[(qi, h)] - m_n)
        run_max[(qi, h)] = m_n
        late.append((qi, lo, hi, h, jnp.exp2(s - m_n.astype(BF16)), alpha))
        if len(late) > PV_BEHIND:
            values(*late.pop(0))
    for job in late:
        values(*job)


def _attn_call(qT, kx, vx, km, casts, mod_job):
    B, A, S = qT.shape
    nb = S // MOBA_BLOCK
    assert nb <= BF16_SUBLANES and nb <= PAIR
    kern = functools.partial(_attn_kernel, n_blocks=nb, n_cast=len(casts), with_mod=mod_job is not None)
    whole = lambda b: (b, 0, 0)
    mod_in, mod_out, mod_shape, mod_args = [], [], [], []
    if mod_job is not None:
        layer, c, w_ada, b_ada3 = mod_job
        assert B >= N_MOD
        mod_in, out, shape = _mod_specs(layer, B, c.shape[1], lambda i: jnp.minimum(i, N_MOD - 1))
        mod_out, mod_shape, mod_args = [out], [shape], [c, w_ada, b_ada3]
    cast_in, cast_out, cast_shape = [], [], []
    for w, layer in casts:
        _, rows, cols = w.shape
        assert rows % (B * BF16_SUBLANES) == 0
        cast_in.append(pl.BlockSpec((None, rows // B, cols), lambda b, layer=layer: (layer, b, 0)))
        cast_out.append(pl.BlockSpec((rows // B, cols), lambda b: (b, 0)))
        cast_shape.append(jax.ShapeDtypeStruct((rows, cols), BF16))
    out = pl.pallas_call(
        kern,
        grid=(B,),
        in_specs=[pl.BlockSpec((1, A, S), whole),
                  pl.BlockSpec((1, S, 2 * A), whole),
                  pl.BlockSpec((1, ATTN_HEADS * V_ROWS, S), whole),
                  pl.BlockSpec((1, nb, A), whole)] + cast_in + mod_in,
        out_specs=[pl.BlockSpec((1, S, A), whole)] + cast_out + mod_out,
        out_shape=[jax.ShapeDtypeStruct((B, S, A), BF16)] + cast_shape + mod_shape,
        compiler_params=_params(1),
        name="moba_attention",
    )(qT, kx, vx, km, *[w for w, _ in casts], *mod_args)
    n_cast = len(casts)
    return out[0], out[1:1 + n_cast], (out[1 + n_cast] if mod_job is not None else None)


def _post_kernel(x_ref, part_ref, ga_ref, ya_ref, mod_ref, n2w_ref,
                 pattn_ref, wout_ref, wg_ref, wu_ref, wd_ref, o_ref, act_ref, *, layer):
    row = pl.ds(pl.program_id(0), 1)
    g1 = mod_ref[2, row, :]
    sh2 = mod_ref[3, row, :]
    sc2 = mod_ref[4, row, :]
    g2 = mod_ref[5, row, :]
    merged = part_ref[0].astype(F32) + ga_ref[0].astype(F32) * _dot(ya_ref[0], pattn_ref[...])
    x1 = x_ref[0] + g1 * _dot(merged.astype(BF16), wout_ref[...])
    ms = jnp.mean(x1 * x1, axis=-1, keepdims=True)
    h2 = (x1 * lax.rsqrt(ms + EPS)) * (n2w_ref[layer:layer + 1, :] * (1.0 + sc2)) + sh2
    h2 = h2.astype(BF16)
    d_ff = wg_ref.shape[1]

    def ff_dots(j):
        cols = slice(j * FF_TN, (j + 1) * FF_TN)
        return _dot(h2, wg_ref[:, cols]), _dot(h2, wu_ref[:, cols])

    res = ff_dots(0)
    for j in range(d_ff // FF_TN):
        nxt = ff_dots(j + 1) if (j + 1) * FF_TN < d_ff else None
        g, u = res
        act_ref[:, j * FF_TN:(j + 1) * FF_TN] = (g * jax.nn.sigmoid(g) * u).astype(BF16)
        res = nxt
    o_ref[0] = x1 + g2 * _dot(act_ref[...], wd_ref[...])


def _post_call(layer, x, part, ga, ya, mod, n2w, pattn, wout, wg, wu, wd):
    B, S, D = x.shape
    A = ya.shape[-1]
    d_ff = wg.shape[-1]
    assert d_ff % FF_TN == 0
    row = lambda b, s: (b, s, 0)
    lspec = functools.partial(_layer_spec, layer)
    return pl.pallas_call(
        functools.partial(_post_kernel, layer=layer),
        grid=(B, S // TM),
        in_specs=[pl.BlockSpec((1, TM, D), row), pl.BlockSpec((1, TM, D), row),
                  pl.BlockSpec((1, TM, D), row), pl.BlockSpec((1, TM, A), row),
                  _const_spec(mod.shape),
                  lspec(n2w.shape),
                  lspec(pattn.shape), lspec(wout.shape), lspec(wg.shape), lspec(wu.shape), lspec(wd.shape)],
        out_specs=pl.BlockSpec((1, TM, D), row),
        out_shape=jax.ShapeDtypeStruct((B, S, D), F32),
        scratch_shapes=[pltpu.VMEM((TM, d_ff), BF16)],
        compiler_params=_params(2),
        name="merge_swiglu",
    )(x, part, ga, ya, mod, n2w, pattn, wout, wg, wu, wd)


def kernel(x, c, positions, norm1_w, norm2_w, w_ada, b_ada, w_in, pool_w, pool_scale, conv_w,
           q_norm_w, k_norm_w, p_pool, p_conv, p_attn, w_out, w_gate, w_up, w_down):
    B, S, D = x.shape
    L = w_ada.shape[0]
    assert S % TM == 0 and TM % SUB == 0 and SUB % MOBA_BLOCK == 0 and HALO >= max(POOL_WINDOWS) - 1
    n_groups, gdim = pool_w.shape[1], pool_w.shape[2]
    assert n_groups == len(POOL_WINDOWS) and conv_w.shape[1] == CONV_K

    assert w_ada.shape[2] == N_MOD * D
    b_ada3 = b_ada.reshape(L, 1, N_MOD * D)
    mod = _modulation(0, c, w_ada, b_ada3)
    cos_t, sin_t = _rope_tables(positions)

    pwbd = jnp.zeros((L, n_groups * gdim, n_groups * gdim), F32)
    for g in range(n_groups):
        pwbd = pwbd.at[:, g * gdim:(g + 1) * gdim, g * gdim:(g + 1) * gdim].set(pool_w[:, g])
    pwbd = pwbd.astype(BF16)
    qw = q_norm_w.reshape(L, HEAD_DIM, 1)
    kw = k_norm_w.reshape(L, HEAD_DIM, 1)
    n1w, n2w, pscale = norm1_w, norm2_w, pool_scale

    in_f32 = (w_in, p_pool, p_conv)
    post_f32 = (p_attn, w_out, w_gate, w_up, w_down)
    in_bf = [w[0].astype(BF16) for w in in_f32]

    for l in range(L):
        part, ga, qT, kx, vx, km = _in_call(l, x, mod, n1w, in_bf[0], qw, kw, cos_t, sin_t, pwbd,
                                            pscale, conv_w, in_bf[1], in_bf[2])
        more = l + 1 < L
        casts = [(w, l) for w in post_f32] + ([(w, l + 1) for w in in_f32] if more else [])
        ya, cast, mod_next = _attn_call(qT, kx, vx, km, casts, (l + 1, c, w_ada, b_ada3) if more else None)
        x = _post_call(l, x, part, ga, ya, mod, n2w, *cast[:len(post_f32)])
        in_bf, mod = cast[len(post_f32):], mod_next
    return x
```

```python
import functools
import math

import numpy as np
import jax
import jax.numpy as jnp
from jax import lax
from jax.experimental import pallas as pl
from jax.experimental.pallas import tpu as pltpu

F32 = jnp.float32
BF16 = jnp.bfloat16

HEAD_DIM = 64
ATTN_HEADS = 8
ATTN_WIDTH = ATTN_HEADS * HEAD_DIM
ROPE_DIM = HEAD_DIM // 4
ROPE_HALF = ROPE_DIM // 2
ROPE_THETA = 500000.0
MOBA_BLOCK = 256
MOBA_TOPK = 3
POOL_WINDOWS = (2, 4, 8, 16)
CONV_K = 3
EPS = 1e-6
NEG = -1e30

LANES = 128
PAIR = LANES
HALO = 16
VMEM_LIMIT = 56 * 1024 * 1024

TM = 512
SUB = TM
FF_TN = 256
GATE_TN = 512
PV_BEHIND = 1
QK_AHEAD = 4
KV_GROUP = 2
BF16_SUBLANES = 16
V_ROWS = HEAD_DIM + BF16_SUBLANES
Q_SCALE = (1.0 / math.sqrt(HEAD_DIM)) * math.log2(math.e)


def _dot(a, b):
    return jnp.dot(a, b, preferred_element_type=F32)


def _const_spec(shape):
    nd = len(shape)
    return pl.BlockSpec(shape, lambda *_: (0,) * nd, pipeline_mode=pl.Buffered(1))


def _layer_spec(layer, shape):
    if len(shape) == 2:
        return _const_spec(shape)
    nd = len(shape)
    return pl.BlockSpec((None,) + tuple(shape[1:]), lambda *_: (layer,) + (0,) * (nd - 1),
                        pipeline_mode=pl.Buffered(1))


def _params(n_axes):
    return pltpu.CompilerParams(dimension_semantics=("arbitrary",) * n_axes,
                                vmem_limit_bytes=VMEM_LIMIT)


N_MOD = 6


def _mod_chunk(c_ref, w_ref, b_ref, o_ref):
    c = c_ref[...]
    c_act = (c * jax.nn.sigmoid(c)).astype(BF16)
    o_ref[...] = _dot(c_act, w_ref[...].astype(BF16)) + b_ref[...]


def _mod_specs(layer, n_batch, d_model, step_to_chunk):
    ins = [pl.BlockSpec((n_batch, d_model), lambda i: (0, 0)),
           pl.BlockSpec((None, d_model, d_model), lambda i: (layer, 0, step_to_chunk(i))),
           pl.BlockSpec((None, 1, d_model), lambda i: (layer, 0, step_to_chunk(i)))]
    out = pl.BlockSpec((None, n_batch, d_model), lambda i: (step_to_chunk(i), 0, 0))
    shape = jax.ShapeDtypeStruct((N_MOD, n_batch, d_model), F32)
    return ins, out, shape


def _cast_specs(casts, n_steps):
    ins, outs, shapes = [], [], []
    for w, layer in casts:
        _, rows, cols = w.shape
        assert rows % (n_steps * BF16_SUBLANES) == 0
        ins.append(pl.BlockSpec((None, rows // n_steps, cols), lambda i, layer=layer: (layer, i, 0)))
        outs.append(pl.BlockSpec((rows // n_steps, cols), lambda i: (i, 0)))
        shapes.append(jax.ShapeDtypeStruct((rows, cols), BF16))
    return ins, outs, shapes


def _prologue_kernel(*refs, n_cast):
    _mod_chunk(*refs[:3], refs[3 + n_cast])
    for w_ref, wb_ref in zip(refs[3:3 + n_cast], refs[4 + n_cast:]):
        wb_ref[...] = w_ref[...].astype(BF16)


def _modulation(layer, c, w_ada, b_ada3, casts):
    B, D = c.shape
    n_steps = max(B, N_MOD)
    ins, out, shape = _mod_specs(layer, B, D, lambda i: jnp.minimum(i, N_MOD - 1))
    cast_in, cast_out, cast_shape = _cast_specs(casts, n_steps)
    res = pl.pallas_call(
        functools.partial(_prologue_kernel, n_cast=len(casts)), grid=(n_steps,),
        in_specs=ins + cast_in, out_specs=[out] + cast_out, out_shape=[shape] + cast_shape,
        compiler_params=_params(1), name="adaln_modulation",
    )(c, w_ada, b_ada3, *[w for w, _ in casts])
    return res[0], res[1:]


def _rope_kernel(pos_ref, freq_ref, c_ref, s_ref):
    ang = freq_ref[...] * pos_ref[0].astype(F32)
    c_ref[0] = jnp.cos(ang)
    s_ref[0] = jnp.sin(ang)


def _rope_tables(positions):
    B, S = positions.shape
    freq = ROPE_THETA ** (-np.arange(0, ROPE_DIM, 2, dtype=np.float64) / ROPE_DIM)
    freq = jnp.asarray(freq.reshape(ROPE_HALF, 1), F32)
    spec = pl.BlockSpec((1, ROPE_HALF, S), lambda b: (b, 0, 0))
    shp = jax.ShapeDtypeStruct((B, ROPE_HALF, S), F32)
    return pl.pallas_call(
        _rope_kernel,
        grid=(B,),
        in_specs=[pl.BlockSpec((1, 1, S), lambda b: (b, 0, 0)),
                  pl.BlockSpec((ROPE_HALF, 1), lambda b: (0, 0))],
        out_specs=(spec, spec),
        out_shape=(shp, shp),
        compiler_params=_params(1),
        name="rotary_tables",
    )(positions.reshape(B, 1, S), freq)


def _in_kernel(x_ref, mod_ref, n1w_ref, win_ref, qw_ref, kw_ref,
               cos_ref, sin_ref, pwbd_ref, pscale_ref, convw_ref, ppool_ref, pconv_ref,
               part_ref, ga_ref, qT_ref, kx_ref, vx_ref, km_ref, ext_ref,
               *, layer, d_model, pool_w, conv_w, attn_w):
    si = pl.program_id(1)
    n_sub = TM // SUB
    hs = {}

    def adaln(sub):
        x = x_ref[0, sub * SUB:(sub + 1) * SUB, :]
        ms = jnp.mean(x * x, axis=-1, keepdims=True)
        sh1 = mod_ref[0, pl.ds(pl.program_id(0), 1), :]
        sc1 = mod_ref[1, pl.ds(pl.program_id(0), 1), :]
        gain = n1w_ref[layer:layer + 1, :]
        hs[sub] = ((x * lax.rsqrt(ms + EPS)) * (gain * (1.0 + sc1)) + sh1).astype(BF16)

    @pl.when(si == 0)
    def _():
        ext_ref[0:HALO, :] = jnp.zeros((HALO, pool_w + conv_w), F32)
        km_ref[0] = jnp.zeros(km_ref.shape[1:], F32)

    o_pool = 0
    o_uc = o_pool + pool_w
    o_bc = o_uc + conv_w
    o_cc = o_bc + conv_w
    o_q = o_cc + conv_w
    o_k = o_q + attn_w
    o_v = o_k + attn_w
    o_g = o_v + attn_w
    n_pairs = attn_w // PAIR

    def sub_stages(sub):
        rows = slice(sub * SUB, (sub + 1) * SUB)
        blk0 = sub * (SUB // MOBA_BLOCK)
        cos_t = cos_ref[0, :, rows]
        sin_t = sin_ref[0, :, rows]

        def norm_rope_t(u, gain_ref):
            gain = gain_ref[...]
            outs = []
            for p in range(n_pairs):
                u_t = u[:, p * PAIR:(p + 1) * PAIR].T
                parts = []
                for hh in range(PAIR // HEAD_DIM):
                    xh = u_t[hh * HEAD_DIM:(hh + 1) * HEAD_DIM, :]
                    xn = xh * lax.rsqrt(jnp.mean(xh * xh, axis=0, keepdims=True) + EPS) * gain
                    x1 = xn[0:ROPE_HALF, :]
                    x2 = xn[ROPE_HALF:ROPE_DIM, :]
                    parts += [x1 * cos_t - x2 * sin_t, x2 * cos_t + x1 * sin_t, xn[ROPE_DIM:, :]]
                outs.append(jnp.concatenate(parts, axis=0))
            return outs

        def proj(lo, hi):
            return lambda: (_dot(hs[sub], win_ref[:, lo:hi]),)

        def q_stage(uq):
            for p, qp_t in enumerate(norm_rope_t(uq, qw_ref)):
                qT_ref[0, p * PAIR:(p + 1) * PAIR, rows] = (qp_t * Q_SCALE).astype(BF16)

        def k_stage(uk):
            row_blk = (si * (TM // MOBA_BLOCK) + blk0
                       + lax.broadcasted_iota(jnp.int32, (SUB, PAIR), 0) // MOBA_BLOCK)
            onehot = jnp.where(lax.broadcasted_iota(jnp.int32, (SUB, PAIR), 1) == row_blk, 1.0, 0.0).astype(BF16)
            for p, kp_t in enumerate(norm_rope_t(uk, kw_ref)):
                kp = kp_t.T
                kx_ref[0, rows, 2 * p * PAIR:(2 * p + 1) * PAIR] = kp.astype(BF16)
                kx_ref[0, rows, (2 * p + 1) * PAIR:(2 * p + 2) * PAIR] = onehot
                for j in range(SUB // MOBA_BLOCK):
                    lanes = slice(p * PAIR, (p + 1) * PAIR)
                    mean = jnp.mean(kp[j * MOBA_BLOCK:(j + 1) * MOBA_BLOCK, :], axis=0, keepdims=True)
                    blk_row = lax.broadcasted_iota(jnp.int32, (km_ref.shape[1], PAIR), 0)
                    mine = blk_row == si * (TM // MOBA_BLOCK) + blk0 + j
                    km_ref[0, :, lanes] = jnp.where(mine, mean, km_ref[0, :, lanes])

        def v_stage(uv):
            ones_rows = jnp.where(lax.broadcasted_iota(jnp.int32, (V_ROWS - HEAD_DIM, SUB), 0) == 0,
                                  1.0, 0.0).astype(BF16)
            for p in range(n_pairs):
                vT = uv[:, p * PAIR:(p + 1) * PAIR].T.astype(BF16)
                for hh in range(PAIR // HEAD_DIM):
                    r0 = (p * (PAIR // HEAD_DIM) + hh) * V_ROWS
                    vx_ref[0, r0:r0 + HEAD_DIM, rows] = vT[hh * HEAD_DIM:(hh + 1) * HEAD_DIM, :]
                    vx_ref[0, r0 + HEAD_DIM:r0 + V_ROWS, rows] = ones_rows

        mixed = {}

        def mixer_stage(ua):
            up = ua[:, o_pool:o_uc]
            uc = ua[:, o_uc:o_bc]
            bc = ua[:, o_bc:o_cc]
            cc = ua[:, o_cc:o_q]
            ext_ref[HALO:HALO + SUB, 0:pool_w] = up
            ext_ref[HALO:HALO + SUB, pool_w:pool_w + conv_w] = cc * uc
            ext = ext_ref[...]
            ext_ref[0:HALO, :] = ext[SUB:SUB + HALO, :]
            e_p = ext[:, 0:pool_w]
            e_c = ext[:, pool_w:pool_w + conv_w]

            s2 = e_p + pltpu.roll(e_p, 1, 0)
            s4 = s2 + pltpu.roll(s2, 2, 0)
            s8 = s4 + pltpu.roll(s4, 4, 0)
            s16 = s8 + pltpu.roll(s8, 8, 0)
            gdim = pool_w // len(POOL_WINDOWS)
            lane = lax.broadcasted_iota(jnp.int32, (1, pool_w), 1)
            wsum = jnp.where(lane < gdim, s2, jnp.where(lane < 2 * gdim, s4, jnp.where(lane < 3 * gdim, s8, s16)))
            win = jnp.where(lane < gdim, POOL_WINDOWS[0],
                            jnp.where(lane < 2 * gdim, POOL_WINDOWS[1],
                                      jnp.where(lane < 3 * gdim, POOL_WINDOWS[2], POOL_WINDOWS[3])))
            t = si * TM + sub * SUB + lax.broadcasted_iota(jnp.int32, (SUB, pool_w), 0)
            cnt = jnp.minimum(t + 1, win).astype(F32)
            pooled = wsum[HALO:, :] / cnt - up
            yp = _dot(pooled.astype(BF16), pwbd_ref[...]) * pscale_ref[layer:layer + 1, :]
            mixed["pool"] = _dot(yp.astype(BF16), ppool_ref[...])

            cw = convw_ref[...]
            conv = cw[0:1, :] * pltpu.roll(e_c, 2, 0) + cw[1:2, :] * pltpu.roll(e_c, 1, 0) + cw[2:3, :] * e_c
            yc = bc * conv[HALO:, :]
            mixed["conv"] = _dot(yc.astype(BF16), pconv_ref[...])

        def gate_dots(c):
            cols = [slice(o_g + i * d_model + c * GATE_TN, o_g + i * d_model + (c + 1) * GATE_TN) for i in range(3)]
            return lambda: tuple(_dot(hs[sub], win_ref[:, cs]) for cs in cols)

        def gate_stage(c):
            out = slice(c * GATE_TN, (c + 1) * GATE_TN)

            def run(g0, g1, g2):
                merged = jax.nn.sigmoid(g0) * mixed["pool"][:, out] + jax.nn.sigmoid(g1) * mixed["conv"][:, out]
                part_ref[0, rows, out] = merged.astype(BF16)
                ga_ref[0, rows, out] = jax.nn.sigmoid(g2).astype(BF16)
            return run

        stages = [(proj(o_q, o_k), q_stage), (proj(o_k, o_v), k_stage)]
        if sub + 1 < n_sub:
            stages.append((lambda: (), lambda: adaln(sub + 1)))
        stages += [(proj(o_v, o_g), v_stage), (proj(o_pool, o_q), mixer_stage)]
        stages += [(gate_dots(c), gate_stage(c)) for c in range(d_model // GATE_TN)]
        return stages

    adaln(0)
    stages = [st for sub in range(n_sub) for st in sub_stages(sub)]
    res = stages[0][0]()
    for i, (_, finish) in enumerate(stages):
        nxt = stages[i + 1][0]() if i + 1 < len(stages) else None
        finish(*res)
        res = nxt


def _in_call(layer, x, mod, n1w, w_in, qw, kw, cos_t, sin_t, pwbd, pscale, convw, ppool, pconv):
    B, S, D = x.shape
    pool_w = pwbd.shape[-1]
    conv_w = convw.shape[-1]
    A = ATTN_WIDTH
    nt = S // TM
    row = lambda b, s: (b, s, 0)
    tr = lambda b, s: (b, 0, s)
    lspec = functools.partial(_layer_spec, layer)
    kern = functools.partial(_in_kernel, layer=layer, d_model=D, pool_w=pool_w, conv_w=conv_w, attn_w=A)
    return pl.pallas_call(
        kern,
        grid=(B, nt),
        in_specs=[pl.BlockSpec((1, TM, D), row),
                  _const_spec(mod.shape),
                  lspec(n1w.shape), lspec(w_in.shape),
                  lspec(qw.shape), lspec(kw.shape),
                  pl.BlockSpec((1, ROPE_HALF, TM), tr), pl.BlockSpec((1, ROPE_HALF, TM), tr),
                  lspec(pwbd.shape), lspec(pscale.shape), lspec(convw.shape),
                  lspec(ppool.shape), lspec(pconv.shape)],
        out_specs=(pl.BlockSpec((1, TM, D), row), pl.BlockSpec((1, TM, D), row),
                   pl.BlockSpec((1, A, TM), tr), pl.BlockSpec((1, TM, 2 * A), row),
                   pl.BlockSpec((1, ATTN_HEADS * V_ROWS, TM), tr),
                   pl.BlockSpec((1, S // MOBA_BLOCK, A), lambda b, s: (b, 0, 0))),
        out_shape=(jax.ShapeDtypeStruct((B, S, D), BF16), jax.ShapeDtypeStruct((B, S, D), BF16),
                   jax.ShapeDtypeStruct((B, A, S), BF16), jax.ShapeDtypeStruct((B, S, 2 * A), BF16),
                   jax.ShapeDtypeStruct((B, ATTN_HEADS * V_ROWS, S), BF16),
                   jax.ShapeDtypeStruct((B, S // MOBA_BLOCK, A), F32)),
        scratch_shapes=[pltpu.VMEM((SUB + HALO, pool_w + conv_w), F32)],
        compiler_params=_params(2),
        name="in_proj_mixers",
    )(x, mod, n1w, w_in, qw, kw, cos_t, sin_t, pwbd, pscale, convw, ppool, pconv)


def _attn_kernel(qT_ref, kx_ref, vx_ref, km_ref, *refs, n_blocks, n_cast, with_mod):
    n_in = n_cast + (3 if with_mod else 0)
    o_ref = refs[n_in]
    for w_ref, wb_ref in zip(refs[:n_cast], refs[n_in + 1:n_in + 1 + n_cast]):
        wb_ref[...] = w_ref[...].astype(BF16)
    if with_mod:
        _mod_chunk(*refs[n_cast:n_in], refs[n_in + 1 + n_cast])

    TQ = MOBA_BLOCK
    heads_per_pair = PAIR // HEAD_DIM
    km = km_ref[0]
    blk = lax.broadcasted_iota(jnp.int32, (n_blocks, TQ), 0)
    half = lax.broadcasted_iota(jnp.int32, (PAIR, TQ), 0) // HEAD_DIM
    bias_pad = jnp.zeros((BF16_SUBLANES - n_blocks, TQ), F32)
    rhs_pad = jnp.zeros((PAIR - BF16_SUBLANES, TQ), BF16)

    def make_rhs(qi, h):
        p, hh = divmod(h, heads_per_pair)
        lanes = slice(p * PAIR, (p + 1) * PAIR)
        qT_pair = qT_ref[0, lanes, qi * TQ:(qi + 1) * TQ]
        qT_h = jnp.where(half == hh, qT_pair, jnp.zeros_like(qT_pair))
        if qi <= MOBA_TOPK:
            sel = blk <= qi
        else:
            past = blk < qi
            km_pair = km[:, lanes]
            km_hi = km_pair.astype(BF16)
            km_lo = (km_pair - km_hi.astype(F32)).astype(BF16)
            gate = _dot(km_hi, qT_h) + _dot(km_lo, qT_h)
            gate = jnp.where(past, gate, NEG)
            beaten = jnp.zeros((n_blocks, TQ), jnp.int32)
            for m in range(n_blocks):
                gm = gate[m:m + 1, :]
                ahead = (gm > gate) | ((gm == gate) & (m < blk))
                beaten = beaten + ahead.astype(jnp.int32)
            sel = (past & (beaten < MOBA_TOPK)) | (blk == qi)
        bias = jnp.concatenate([jnp.where(sel, 0.0, NEG), bias_pad], axis=0).astype(BF16)
        return jnp.concatenate([qT_h, bias, rhs_pad], axis=0)

    items = []
    for qi in range(n_blocks):
        hi = qi
        while hi >= 0:
            lo = max(hi - KV_GROUP + 1, 0)
            items += [(qi, lo, hi, h) for h in range(ATTN_HEADS)]
            hi = lo - 1
    rhs = {}
    run_max = {}
    run_acc = {}
    done = {}

    def causal_mask(n_keys):
        kpos = lax.broadcasted_iota(jnp.int32, (n_keys, TQ), 0)
        qpos = lax.broadcasted_iota(jnp.int32, (n_keys, TQ), 1)
        return kpos <= qpos + (n_keys - TQ)

    def scores(item):
        qi, lo, hi, h = item
        if (qi, h) not in rhs:
            rhs[(qi, h)] = make_rhs(qi, h)
        p = h // heads_per_pair
        s = _dot(kx_ref[0, lo * TQ:(hi + 1) * TQ, 2 * p * PAIR:(2 * p + 2) * PAIR], rhs[(qi, h)])
        return s.astype(BF16)

    def values(qi, lo, hi, h, pr, alpha):
        acc = _dot(vx_ref[0, h * V_ROWS:(h + 1) * V_ROWS, lo * TQ:(hi + 1) * TQ], pr)
        if alpha is not None:
            acc = alpha * run_acc[(qi, h)] + acc
        run_acc[(qi, h)] = acc
        if lo == 0:
            done[h] = acc[0:HEAD_DIM, :] / acc[HEAD_DIM:HEAD_DIM + 1, :]
            del run_acc[(qi, h)], run_max[(qi, h)], rhs[(qi, h)]
            if h == ATTN_HEADS - 1:
                oT = jnp.concatenate([done[j] for j in range(ATTN_HEADS)], axis=0)
                o_ref[0, qi * TQ:(qi + 1) * TQ, :] = oT.T.astype(BF16)

    pending = [scores(it) for it in items[:QK_AHEAD]]
    late = []
    for i, (qi, lo, hi, h) in enumerate(items):
        if i + QK_AHEAD < len(items):
            pending.append(scores(items[i + QK_AHEAD]))
        s = pending.pop(0)
        if hi == qi:
            s = jnp.where(causal_mask((hi - lo + 1) * TQ), s, jnp.asarray(NEG, BF16))
        m_n = jnp.max(s, axis=0, keepdims=True).astype(F32)
        alpha = None
        if (qi, h) in run_max:
            m_n = jnp.maximum(run_max[(qi, h)], m_n)
            alpha = jnp.exp2(run_max[(qi, h)] - m_n)
        run_max[(qi, h)] = m_n
        late.append((qi, lo, hi, h, jnp.exp2(s - m_n.astype(BF16)), alpha))
        if len(late) > PV_BEHIND:
            values(*late.pop(0))
    for job in late:
        values(*job)


def _attn_call(qT, kx, vx, km, casts, mod_job):
    B, A, S = qT.shape
    nb = S // MOBA_BLOCK
    assert nb <= BF16_SUBLANES and nb <= PAIR
    kern = functools.partial(_attn_kernel, n_blocks=nb, n_cast=len(casts), with_mod=mod_job is not None)
    whole = lambda b: (b, 0, 0)
    mod_in, mod_out, mod_shape, mod_args = [], [], [], []
    if mod_job is not None:
        layer, c, w_ada, b_ada3 = mod_job
        assert B >= N_MOD
        mod_in, out, shape = _mod_specs(layer, B, c.shape[1], lambda i: jnp.minimum(i, N_MOD - 1))
        mod_out, mod_shape, mod_args = [out], [shape], [c, w_ada, b_ada3]
    cast_in, cast_out, cast_shape = _cast_specs(casts, B)
    out = pl.pallas_call(
        kern,
        grid=(B,),
        in_specs=[pl.BlockSpec((1, A, S), whole),
                  pl.BlockSpec((1, S, 2 * A), whole),
                  pl.BlockSpec((1, ATTN_HEADS * V_ROWS, S), whole),
                  pl.BlockSpec((1, nb, A), whole)] + cast_in + mod_in,
        out_specs=[pl.BlockSpec((1, S, A), whole)] + cast_out + mod_out,
        out_shape=[jax.ShapeDtypeStruct((B, S, A), BF16)] + cast_shape + mod_shape,
        compiler_params=_params(1),
        name="moba_attention",
    )(qT, kx, vx, km, *[w for w, _ in casts], *mod_args)
    n_cast = len(casts)
    return out[0], out[1:1 + n_cast], (out[1 + n_cast] if mod_job is not None else None)


def _post_kernel(x_ref, part_ref, ga_ref, ya_ref, mod_ref, n2w_ref,
                 pattn_ref, wout_ref, wg_ref, wu_ref, wd_ref, o_ref, act_ref, *, layer):
    row = pl.ds(pl.program_id(0), 1)
    g1 = mod_ref[2, row, :]
    sh2 = mod_ref[3, row, :]
    sc2 = mod_ref[4, row, :]
    g2 = mod_ref[5, row, :]
    merged = part_ref[0].astype(F32) + ga_ref[0].astype(F32) * _dot(ya_ref[0], pattn_ref[...])
    x1 = x_ref[0] + g1 * _dot(merged.astype(BF16), wout_ref[...])
    ms = jnp.mean(x1 * x1, axis=-1, keepdims=True)
    h2 = (x1 * lax.rsqrt(ms + EPS)) * (n2w_ref[layer:layer + 1, :] * (1.0 + sc2)) + sh2
    h2 = h2.astype(BF16)
    d_ff = wg_ref.shape[1]

    def ff_dots(j):
        cols = slice(j * FF_TN, (j + 1) * FF_TN)
        return _dot(h2, wg_ref[:, cols]), _dot(h2, wu_ref[:, cols])

    res = ff_dots(0)
    for j in range(d_ff // FF_TN):
        nxt = ff_dots(j + 1) if (j + 1) * FF_TN < d_ff else None
        g, u = res
        act_ref[:, j * FF_TN:(j + 1) * FF_TN] = (g * jax.nn.sigmoid(g) * u).astype(BF16)
        res = nxt
    o_ref[0] = x1 + g2 * _dot(act_ref[...], wd_ref[...])


def _post_call(layer, x, part, ga, ya, mod, n2w, pattn, wout, wg, wu, wd):
    B, S, D = x.shape
    A = ya.shape[-1]
    d_ff = wg.shape[-1]
    assert d_ff % FF_TN == 0
    row = lambda b, s: (b, s, 0)
    lspec = functools.partial(_layer_spec, layer)
    return pl.pallas_call(
        functools.partial(_post_kernel, layer=layer),
        grid=(B, S // TM),
        in_specs=[pl.BlockSpec((1, TM, D), row), pl.BlockSpec((1, TM, D), row),
                  pl.BlockSpec((1, TM, D), row), pl.BlockSpec((1, TM, A), row),
                  _const_spec(mod.shape),
                  lspec(n2w.shape),
                  lspec(pattn.shape), lspec(wout.shape), lspec(wg.shape), lspec(wu.shape), lspec(wd.shape)],
        out_specs=pl.BlockSpec((1, TM, D), row),
        out_shape=jax.ShapeDtypeStruct((B, S, D), F32),
        scratch_shapes=[pltpu.VMEM((TM, d_ff), BF16)],
        compiler_params=_params(2),
        name="merge_swiglu",
    )(x, part, ga, ya, mod, n2w, pattn, wout, wg, wu, wd)


def kernel(x, c, positions, norm1_w, norm2_w, w_ada, b_ada, w_in, pool_w, pool_scale, conv_w,
           q_norm_w, k_norm_w, p_pool, p_conv, p_attn, w_out, w_gate, w_up, w_down):
    B, S, D = x.shape
    L = w_ada.shape[0]
    assert S % TM == 0 and TM % SUB == 0 and SUB % MOBA_BLOCK == 0 and HALO >= max(POOL_WINDOWS) - 1
    n_groups, gdim = pool_w.shape[1], pool_w.shape[2]
    assert n_groups == len(POOL_WINDOWS) and conv_w.shape[1] == CONV_K

    assert w_ada.shape[2] == N_MOD * D
    b_ada3 = b_ada.reshape(L, 1, N_MOD * D)
    in_f32 = (w_in, p_pool, p_conv)
    post_f32 = (p_attn, w_out, w_gate, w_up, w_down)
    mod, in_bf = _modulation(0, c, w_ada, b_ada3, [(w, 0) for w in in_f32])
    cos_t, sin_t = _rope_tables(positions)

    pwbd = jnp.zeros((L, n_groups * gdim, n_groups * gdim), F32)
    for g in range(n_groups):
        pwbd = pwbd.at[:, g * gdim:(g + 1) * gdim, g * gdim:(g + 1) * gdim].set(pool_w[:, g])
    pwbd = pwbd.astype(BF16)
    qw = q_norm_w.reshape(L, HEAD_DIM, 1)
    kw = k_norm_w.reshape(L, HEAD_DIM, 1)
    n1w, n2w, pscale = norm1_w, norm2_w, pool_scale

    for l in range(L):
        part, ga, qT, kx, vx, km = _in_call(l, x, mod, n1w, in_bf[0], qw, kw, cos_t, sin_t, pwbd,
                                            pscale, conv_w, in_bf[1], in_bf[2])
        more = l + 1 < L
        casts = [(w, l) for w in post_f32] + ([(w, l + 1) for w in in_f32] if more else [])
        ya, cast, mod_next = _attn_call(qT, kx, vx, km, casts, (l + 1, c, w_ada, b_ada3) if more else None)
        x = _post_call(l, x, part, ga, ya, mod, n2w, *cast[:len(post_f32)])
        in_bf, mod = cast[len(post_f32):], mod_next
    return x
```

```python
import functools
import math

import numpy as np
import jax
import jax.numpy as jnp
from jax import lax
from jax.experimental import pallas as pl
from jax.experimental.pallas import tpu as pltpu

F32 = jnp.float32
BF16 = jnp.bfloat16

HEAD_DIM = 64
ATTN_HEADS = 8
ATTN_WIDTH = ATTN_HEADS * HEAD_DIM
ROPE_DIM = HEAD_DIM // 4
ROPE_HALF = ROPE_DIM // 2
ROPE_THETA = 500000.0
MOBA_BLOCK = 256
MOBA_TOPK = 3
POOL_WINDOWS = (2, 4, 8, 16)
CONV_K = 3
EPS = 1e-6
NEG = -1e30

LANES = 128
PAIR = LANES
HALO = 16
VMEM_LIMIT = 56 * 1024 * 1024

TM = 512
SUB = TM
FF_TN = 256
FF_AHEAD = 2
GATE_TN = 512
PV_BEHIND = 1
QK_AHEAD = 4
KV_GROUP = 2
BF16_SUBLANES = 16
V_ROWS = HEAD_DIM + BF16_SUBLANES
Q_SCALE = (1.0 / math.sqrt(HEAD_DIM)) * math.log2(math.e)


def _dot(a, b):
    return jnp.dot(a, b, preferred_element_type=F32)


def _const_spec(shape):
    nd = len(shape)
    return pl.BlockSpec(shape, lambda *_: (0,) * nd, pipeline_mode=pl.Buffered(1))


def _layer_spec(layer, shape):
    if len(shape) == 2:
        return _const_spec(shape)
    nd = len(shape)
    return pl.BlockSpec((None,) + tuple(shape[1:]), lambda *_: (layer,) + (0,) * (nd - 1),
                        pipeline_mode=pl.Buffered(1))


def _params(n_axes):
    return pltpu.CompilerParams(dimension_semantics=("arbitrary",) * n_axes,
                                vmem_limit_bytes=VMEM_LIMIT)


N_MOD = 6


def _mod_chunk(c_ref, w_ref, b_ref, o_ref):
    c = c_ref[...]
    c_act = (c * jax.nn.sigmoid(c)).astype(BF16)
    o_ref[...] = _dot(c_act, w_ref[...].astype(BF16)) + b_ref[...]


def _mod_specs(layer, n_batch, d_model, step_to_chunk):
    ins = [pl.BlockSpec((n_batch, d_model), lambda i: (0, 0)),
           pl.BlockSpec((None, d_model, d_model), lambda i: (layer, 0, step_to_chunk(i))),
           pl.BlockSpec((None, 1, d_model), lambda i: (layer, 0, step_to_chunk(i)))]
    out = pl.BlockSpec((None, n_batch, d_model), lambda i: (step_to_chunk(i), 0, 0))
    shape = jax.ShapeDtypeStruct((N_MOD, n_batch, d_model), F32)
    return ins, out, shape


def _cast_specs(casts, n_steps):
    ins, outs, shapes = [], [], []
    for w, layer in casts:
        _, rows, cols = w.shape
        assert rows % (n_steps * BF16_SUBLANES) == 0
        ins.append(pl.BlockSpec((None, rows // n_steps, cols), lambda i, layer=layer: (layer, i, 0)))
        outs.append(pl.BlockSpec((rows // n_steps, cols), lambda i: (i, 0)))
        shapes.append(jax.ShapeDtypeStruct((rows, cols), BF16))
    return ins, outs, shapes


def _prologue_kernel(*refs, n_cast):
    _mod_chunk(*refs[:3], refs[3 + n_cast])
    for w_ref, wb_ref in zip(refs[3:3 + n_cast], refs[4 + n_cast:]):
        wb_ref[...] = w_ref[...].astype(BF16)


def _modulation(layer, c, w_ada, b_ada3, casts):
    B, D = c.shape
    n_steps = max(B, N_MOD)
    ins, out, shape = _mod_specs(layer, B, D, lambda i: jnp.minimum(i, N_MOD - 1))
    cast_in, cast_out, cast_shape = _cast_specs(casts, n_steps)
    res = pl.pallas_call(
        functools.partial(_prologue_kernel, n_cast=len(casts)), grid=(n_steps,),
        in_specs=ins + cast_in, out_specs=[out] + cast_out, out_shape=[shape] + cast_shape,
        compiler_params=_params(1), name="adaln_modulation",
    )(c, w_ada, b_ada3, *[w for w, _ in casts])
    return res[0], res[1:]


def _rope_kernel(pos_ref, freq_ref, c_ref, s_ref):
    ang = freq_ref[...] * pos_ref[0].astype(F32)
    c_ref[0] = jnp.cos(ang)
    s_ref[0] = jnp.sin(ang)


def _rope_tables(positions):
    B, S = positions.shape
    freq = ROPE_THETA ** (-np.arange(0, ROPE_DIM, 2, dtype=np.float64) / ROPE_DIM)
    freq = jnp.asarray(freq.reshape(ROPE_HALF, 1), F32)
    spec = pl.BlockSpec((1, ROPE_HALF, S), lambda b: (b, 0, 0))
    shp = jax.ShapeDtypeStruct((B, ROPE_HALF, S), F32)
    return pl.pallas_call(
        _rope_kernel,
        grid=(B,),
        in_specs=[pl.BlockSpec((1, 1, S), lambda b: (b, 0, 0)),
                  pl.BlockSpec((ROPE_HALF, 1), lambda b: (0, 0))],
        out_specs=(spec, spec),
        out_shape=(shp, shp),
        compiler_params=_params(1),
        name="rotary_tables",
    )(positions.reshape(B, 1, S), freq)


def _in_kernel(x_ref, mod_ref, n1w_ref, win_ref, qw_ref, kw_ref,
               cos_ref, sin_ref, pwbd_ref, pscale_ref, convw_ref, ppool_ref, pconv_ref,
               part_ref, ga_ref, qT_ref, kx_ref, vx_ref, km_ref, ext_ref,
               *, layer, d_model, pool_w, conv_w, attn_w):
    si = pl.program_id(1)
    n_sub = TM // SUB
    hs = {}

    def adaln(sub):
        x = x_ref[0, sub * SUB:(sub + 1) * SUB, :]
        ms = jnp.mean(x * x, axis=-1, keepdims=True)
        sh1 = mod_ref[0, pl.ds(pl.program_id(0), 1), :]
        sc1 = mod_ref[1, pl.ds(pl.program_id(0), 1), :]
        gain = n1w_ref[layer:layer + 1, :]
        hs[sub] = ((x * lax.rsqrt(ms + EPS)) * (gain * (1.0 + sc1)) + sh1).astype(BF16)

    @pl.when(si == 0)
    def _():
        ext_ref[0:HALO, :] = jnp.zeros((HALO, pool_w + conv_w), F32)
        km_ref[0] = jnp.zeros(km_ref.shape[1:], F32)

    o_pool = 0
    o_uc = o_pool + pool_w
    o_bc = o_uc + conv_w
    o_cc = o_bc + conv_w
    o_q = o_cc + conv_w
    o_k = o_q + attn_w
    o_v = o_k + attn_w
    o_g = o_v + attn_w
    n_pairs = attn_w // PAIR

    def sub_stages(sub):
        rows = slice(sub * SUB, (sub + 1) * SUB)
        blk0 = sub * (SUB // MOBA_BLOCK)
        cos_t = cos_ref[0, :, rows]
        sin_t = sin_ref[0, :, rows]

        def norm_rope_t(u, gain_ref):
            gain = gain_ref[...]
            outs = []
            for p in range(n_pairs):
                u_t = u[:, p * PAIR:(p + 1) * PAIR].T
                parts = []
                for hh in range(PAIR // HEAD_DIM):
                    xh = u_t[hh * HEAD_DIM:(hh + 1) * HEAD_DIM, :]
                    xn = xh * lax.rsqrt(jnp.mean(xh * xh, axis=0, keepdims=True) + EPS) * gain
                    x1 = xn[0:ROPE_HALF, :]
                    x2 = xn[ROPE_HALF:ROPE_DIM, :]
                    parts += [x1 * cos_t - x2 * sin_t, x2 * cos_t + x1 * sin_t, xn[ROPE_DIM:, :]]
                outs.append(jnp.concatenate(parts, axis=0))
            return outs

        def proj(lo, hi):
            return lambda: (_dot(hs[sub], win_ref[:, lo:hi]),)

        def q_stage(uq):
            for p, qp_t in enumerate(norm_rope_t(uq, qw_ref)):
                qT_ref[0, p * PAIR:(p + 1) * PAIR, rows] = (qp_t * Q_SCALE).astype(BF16)

        def k_stage(uk):
            row_blk = (si * (TM // MOBA_BLOCK) + blk0
                       + lax.broadcasted_iota(jnp.int32, (SUB, PAIR), 0) // MOBA_BLOCK)
            onehot = jnp.where(lax.broadcasted_iota(jnp.int32, (SUB, PAIR), 1) == row_blk, 1.0, 0.0).astype(BF16)
            for p, kp_t in enumerate(norm_rope_t(uk, kw_ref)):
                kp = kp_t.T
                kx_ref[0, rows, 2 * p * PAIR:(2 * p + 1) * PAIR] = kp.astype(BF16)
                kx_ref[0, rows, (2 * p + 1) * PAIR:(2 * p + 2) * PAIR] = onehot
                for j in range(SUB // MOBA_BLOCK):
                    lanes = slice(p * PAIR, (p + 1) * PAIR)
                    mean = jnp.mean(kp[j * MOBA_BLOCK:(j + 1) * MOBA_BLOCK, :], axis=0, keepdims=True)
                    blk_row = lax.broadcasted_iota(jnp.int32, (km_ref.shape[1], PAIR), 0)
                    mine = blk_row == si * (TM // MOBA_BLOCK) + blk0 + j
                    km_ref[0, :, lanes] = jnp.where(mine, mean, km_ref[0, :, lanes])

        def v_stage(uv):
            ones_rows = jnp.where(lax.broadcasted_iota(jnp.int32, (V_ROWS - HEAD_DIM, SUB), 0) == 0,
                                  1.0, 0.0).astype(BF16)
            for p in range(n_pairs):
                vT = uv[:, p * PAIR:(p + 1) * PAIR].T.astype(BF16)
                for hh in range(PAIR // HEAD_DIM):
                    r0 = (p * (PAIR // HEAD_DIM) + hh) * V_ROWS
                    vx_ref[0, r0:r0 + HEAD_DIM, rows] = vT[hh * HEAD_DIM:(hh + 1) * HEAD_DIM, :]
                    vx_ref[0, r0 + HEAD_DIM:r0 + V_ROWS, rows] = ones_rows

        mixed = {}

        def mixer_stage(ua):
            up = ua[:, o_pool:o_uc]
            uc = ua[:, o_uc:o_bc]
            bc = ua[:, o_bc:o_cc]
            cc = ua[:, o_cc:o_q]
            ext_ref[HALO:HALO + SUB, 0:pool_w] = up
            ext_ref[HALO:HALO + SUB, pool_w:pool_w + conv_w] = cc * uc
            ext = ext_ref[...]
            ext_ref[0:HALO, :] = ext[SUB:SUB + HALO, :]
            e_p = ext[:, 0:pool_w]
            e_c = ext[:, pool_w:pool_w + conv_w]

            s2 = e_p + pltpu.roll(e_p, 1, 0)
            s4 = s2 + pltpu.roll(s2, 2, 0)
            s8 = s4 + pltpu.roll(s4, 4, 0)
            s16 = s8 + pltpu.roll(s8, 8, 0)
            gdim = pool_w // len(POOL_WINDOWS)
            lane = lax.broadcasted_iota(jnp.int32, (1, pool_w), 1)
            wsum = jnp.where(lane < gdim, s2, jnp.where(lane < 2 * gdim, s4, jnp.where(lane < 3 * gdim, s8, s16)))
            win = jnp.where(lane < gdim, POOL_WINDOWS[0],
                            jnp.where(lane < 2 * gdim, POOL_WINDOWS[1],
                                      jnp.where(lane < 3 * gdim, POOL_WINDOWS[2], POOL_WINDOWS[3])))
            t = si * TM + sub * SUB + lax.broadcasted_iota(jnp.int32, (SUB, pool_w), 0)
            cnt = jnp.minimum(t + 1, win).astype(F32)
            pooled = wsum[HALO:, :] / cnt - up
            yp = _dot(pooled.astype(BF16), pwbd_ref[...]) * pscale_ref[layer:layer + 1, :]
            mixed["pool"] = _dot(yp.astype(BF16), ppool_ref[...])

            cw = convw_ref[...]
            conv = cw[0:1, :] * pltpu.roll(e_c, 2, 0) + cw[1:2, :] * pltpu.roll(e_c, 1, 0) + cw[2:3, :] * e_c
            yc = bc * conv[HALO:, :]
            mixed["conv"] = _dot(yc.astype(BF16), pconv_ref[...])

        def gate_dots(c):
            cols = [slice(o_g + i * d_model + c * GATE_TN, o_g + i * d_model + (c + 1) * GATE_TN) for i in range(3)]
            return lambda: tuple(_dot(hs[sub], win_ref[:, cs]) for cs in cols)

        def gate_stage(c):
            out = slice(c * GATE_TN, (c + 1) * GATE_TN)

            def run(g0, g1, g2):
                merged = jax.nn.sigmoid(g0) * mixed["pool"][:, out] + jax.nn.sigmoid(g1) * mixed["conv"][:, out]
                part_ref[0, rows, out] = merged.astype(BF16)
                ga_ref[0, rows, out] = jax.nn.sigmoid(g2).astype(BF16)
            return run

        stages = [(proj(o_q, o_k), q_stage), (proj(o_k, o_v), k_stage)]
        if sub + 1 < n_sub:
            stages.append((lambda: (), lambda: adaln(sub + 1)))
        stages += [(proj(o_v, o_g), v_stage), (proj(o_pool, o_q), mixer_stage)]
        stages += [(gate_dots(c), gate_stage(c)) for c in range(d_model // GATE_TN)]
        return stages

    adaln(0)
    stages = [st for sub in range(n_sub) for st in sub_stages(sub)]
    res = stages[0][0]()
    for i, (_, finish) in enumerate(stages):
        nxt = stages[i + 1][0]() if i + 1 < len(stages) else None
        finish(*res)
        res = nxt


def _in_call(layer, x, mod, n1w, w_in, qw, kw, cos_t, sin_t, pwbd, pscale, convw, ppool, pconv):
    B, S, D = x.shape
    pool_w = pwbd.shape[-1]
    conv_w = convw.shape[-1]
    A = ATTN_WIDTH
    nt = S // TM
    row = lambda b, s: (b, s, 0)
    tr = lambda b, s: (b, 0, s)
    lspec = functools.partial(_layer_spec, layer)
    kern = functools.partial(_in_kernel, layer=layer, d_model=D, pool_w=pool_w, conv_w=conv_w, attn_w=A)
    return pl.pallas_call(
        kern,
        grid=(B, nt),
        in_specs=[pl.BlockSpec((1, TM, D), row),
                  _const_spec(mod.shape),
                  lspec(n1w.shape), lspec(w_in.shape),
                  lspec(qw.shape), lspec(kw.shape),
                  pl.BlockSpec((1, ROPE_HALF, TM), tr), pl.BlockSpec((1, ROPE_HALF, TM), tr),
                  lspec(pwbd.shape), lspec(pscale.shape), lspec(convw.shape),
                  lspec(ppool.shape), lspec(pconv.shape)],
        out_specs=(pl.BlockSpec((1, TM, D), row), pl.BlockSpec((1, TM, D), row),
                   pl.BlockSpec((1, A, TM), tr), pl.BlockSpec((1, TM, 2 * A), row),
                   pl.BlockSpec((1, ATTN_HEADS * V_ROWS, TM), tr),
                   pl.BlockSpec((1, S // MOBA_BLOCK, A), lambda b, s: (b, 0, 0))),
        out_shape=(jax.ShapeDtypeStruct((B, S, D), BF16), jax.ShapeDtypeStruct((B, S, D), BF16),
                   jax.ShapeDtypeStruct((B, A, S), BF16), jax.ShapeDtypeStruct((B, S, 2 * A), BF16),
                   jax.ShapeDtypeStruct((B, ATTN_HEADS * V_ROWS, S), BF16),
                   jax.ShapeDtypeStruct((B, S // MOBA_BLOCK, A), F32)),
        scratch_shapes=[pltpu.VMEM((SUB + HALO, pool_w + conv_w), F32)],
        compiler_params=_params(2),
        name="in_proj_mixers",
    )(x, mod, n1w, w_in, qw, kw, cos_t, sin_t, pwbd, pscale, convw, ppool, pconv)


def _attn_kernel(qT_ref, kx_ref, vx_ref, km_ref, *refs, n_blocks, n_cast, with_mod):
    n_in = n_cast + (3 if with_mod else 0)
    o_ref = refs[n_in]
    for w_ref, wb_ref in zip(refs[:n_cast], refs[n_in + 1:n_in + 1 + n_cast]):
        wb_ref[...] = w_ref[...].astype(BF16)
    if with_mod:
        _mod_chunk(*refs[n_cast:n_in], refs[n_in + 1 + n_cast])

    TQ = MOBA_BLOCK
    heads_per_pair = PAIR // HEAD_DIM
    km = km_ref[0]
    blk = lax.broadcasted_iota(jnp.int32, (n_blocks, TQ), 0)
    half = lax.broadcasted_iota(jnp.int32, (PAIR, TQ), 0) // HEAD_DIM
    bias_pad = jnp.zeros((BF16_SUBLANES - n_blocks, TQ), F32)
    rhs_pad = jnp.zeros((PAIR - BF16_SUBLANES, TQ), BF16)

    def make_rhs(qi, h):
        p, hh = divmod(h, heads_per_pair)
        lanes = slice(p * PAIR, (p + 1) * PAIR)
        qT_pair = qT_ref[0, lanes, qi * TQ:(qi + 1) * TQ]
        qT_h = jnp.where(half == hh, qT_pair, jnp.zeros_like(qT_pair))
        if qi <= MOBA_TOPK:
            sel = blk <= qi
        else:
            past = blk < qi
            km_pair = km[:, lanes]
            km_hi = km_pair.astype(BF16)
            km_lo = (km_pair - km_hi.astype(F32)).astype(BF16)
            gate = _dot(km_hi, qT_h) + _dot(km_lo, qT_h)
            gate = jnp.where(past, gate, NEG)
            beaten = jnp.zeros((n_blocks, TQ), jnp.int32)
            for m in range(n_blocks):
                gm = gate[m:m + 1, :]
                ahead = (gm > gate) | ((gm == gate) & (m < blk))
                beaten = beaten + ahead.astype(jnp.int32)
            sel = (past & (beaten < MOBA_TOPK)) | (blk == qi)
        bias = jnp.concatenate([jnp.where(sel, 0.0, NEG), bias_pad], axis=0).astype(BF16)
        return jnp.concatenate([qT_h, bias, rhs_pad], axis=0)

    items = []
    for qi in range(n_blocks):
        hi = qi
        while hi >= 0:
            lo = max(hi - KV_GROUP + 1, 0)
            items += [(qi, lo, hi, h) for h in range(ATTN_HEADS)]
            hi = lo - 1
    rhs = {}
    run_max = {}
    run_acc = {}
    done = {}

    def causal_mask(n_keys):
        kpos = lax.broadcasted_iota(jnp.int32, (n_keys, TQ), 0)
        qpos = lax.broadcasted_iota(jnp.int32, (n_keys, TQ), 1)
        return kpos <= qpos + (n_keys - TQ)

    def scores(item):
        qi, lo, hi, h = item
        if (qi, h) not in rhs:
            rhs[(qi, h)] = make_rhs(qi, h)
        p = h // heads_per_pair
        s = _dot(kx_ref[0, lo * TQ:(hi + 1) * TQ, 2 * p * PAIR:(2 * p + 2) * PAIR], rhs[(qi, h)])
        return s.astype(BF16)

    def values(qi, lo, hi, h, pr, alpha):
        acc = _dot(vx_ref[0, h * V_ROWS:(h + 1) * V_ROWS, lo * TQ:(hi + 1) * TQ], pr)
        if alpha is not None:
            acc = alpha * run_acc[(qi, h)] + acc
        run_acc[(qi, h)] = acc
        if lo == 0:
            done[h] = acc[0:HEAD_DIM, :] / acc[HEAD_DIM:HEAD_DIM + 1, :]
            del run_acc[(qi, h)], run_max[(qi, h)], rhs[(qi, h)]
            if h == ATTN_HEADS - 1:
                oT = jnp.concatenate([done[j] for j in range(ATTN_HEADS)], axis=0)
                o_ref[0, qi * TQ:(qi + 1) * TQ, :] = oT.T.astype(BF16)

    pending = [scores(it) for it in items[:QK_AHEAD]]
    late = []
    for i, (qi, lo, hi, h) in enumerate(items):
        if i + QK_AHEAD < len(items):
            pending.append(scores(items[i + QK_AHEAD]))
        s = pending.pop(0)
        if hi == qi:
            s = jnp.where(causal_mask((hi - lo + 1) * TQ), s, jnp.asarray(NEG, BF16))
        m_n = jnp.max(s, axis=0, keepdims=True).astype(F32)
        alpha = None
        if (qi, h) in run_max:
            m_n = jnp.maximum(run_max[(qi, h)], m_n)
            alpha = jnp.exp2(run_max[(qi, h)] - m_n)
        run_max[(qi, h)] = m_n
        late.append((qi, lo, hi, h, jnp.exp2(s - m_n.astype(BF16)), alpha))
        if len(late) > PV_BEHIND:
            values(*late.pop(0))
    for job in late:
        values(*job)


def _attn_call(qT, kx, vx, km, casts, mod_job):
    B, A, S = qT.shape
    nb = S // MOBA_BLOCK
    assert nb <= BF16_SUBLANES and nb <= PAIR
    kern = functools.partial(_attn_kernel, n_blocks=nb, n_cast=len(casts), with_mod=mod_job is not None)
    whole = lambda b: (b, 0, 0)
    mod_in, mod_out, mod_shape, mod_args = [], [], [], []
    if mod_job is not None:
        layer, c, w_ada, b_ada3 = mod_job
        assert B >= N_MOD
        mod_in, out, shape = _mod_specs(layer, B, c.shape[1], lambda i: jnp.minimum(i, N_MOD - 1))
        mod_out, mod_shape, mod_args = [out], [shape], [c, w_ada, b_ada3]
    cast_in, cast_out, cast_shape = _cast_specs(casts, B)
    out = pl.pallas_call(
        kern,
        grid=(B,),
        in_specs=[pl.BlockSpec((1, A, S), whole),
                  pl.BlockSpec((1, S, 2 * A), whole),
                  pl.BlockSpec((1, ATTN_HEADS * V_ROWS, S), whole),
                  pl.BlockSpec((1, nb, A), whole)] + cast_in + mod_in,
        out_specs=[pl.BlockSpec((1, S, A), whole)] + cast_out + mod_out,
        out_shape=[jax.ShapeDtypeStruct((B, S, A), BF16)] + cast_shape + mod_shape,
        compiler_params=_params(1),
        name="moba_attention",
    )(qT, kx, vx, km, *[w for w, _ in casts], *mod_args)
    n_cast = len(casts)
    return out[0], out[1:1 + n_cast], (out[1 + n_cast] if mod_job is not None else None)


def _post_kernel(x_ref, part_ref, ga_ref, ya_ref, mod_ref, n2w_ref,
                 pattn_ref, wout_ref, wg_ref, wu_ref, wd_ref, o_ref, act_ref, *, layer):
    row = pl.ds(pl.program_id(0), 1)
    g1 = mod_ref[2, row, :]
    sh2 = mod_ref[3, row, :]
    sc2 = mod_ref[4, row, :]
    g2 = mod_ref[5, row, :]
    merged = part_ref[0].astype(F32) + ga_ref[0].astype(F32) * _dot(ya_ref[0], pattn_ref[...])
    x1 = x_ref[0] + g1 * _dot(merged.astype(BF16), wout_ref[...])
    ms = jnp.mean(x1 * x1, axis=-1, keepdims=True)
    h2 = (x1 * lax.rsqrt(ms + EPS)) * (n2w_ref[layer:layer + 1, :] * (1.0 + sc2)) + sh2
    h2 = h2.astype(BF16)
    d_ff = wg_ref.shape[1]

    def ff_dots(j):
        cols = slice(j * FF_TN, (j + 1) * FF_TN)
        return _dot(h2, wg_ref[:, cols]), _dot(h2, wu_ref[:, cols])

    n_chunks = d_ff // FF_TN
    pending = [ff_dots(j) for j in range(min(FF_AHEAD, n_chunks))]
    for j in range(n_chunks):
        if j + FF_AHEAD < n_chunks:
            pending.append(ff_dots(j + FF_AHEAD))
        g, u = pending.pop(0)
        act_ref[:, j * FF_TN:(j + 1) * FF_TN] = (g * jax.nn.sigmoid(g) * u).astype(BF16)
    o_ref[0] = x1 + g2 * _dot(act_ref[...], wd_ref[...])


def _post_call(layer, x, part, ga, ya, mod, n2w, pattn, wout, wg, wu, wd):
    B, S, D = x.shape
    A = ya.shape[-1]
    d_ff = wg.shape[-1]
    assert d_ff % FF_TN == 0
    row = lambda b, s: (b, s, 0)
    lspec = functools.partial(_layer_spec, layer)
    return pl.pallas_call(
        functools.partial(_post_kernel, layer=layer),
        grid=(B, S // TM),
        in_specs=[pl.BlockSpec((1, TM, D), row), pl.BlockSpec((1, TM, D), row),
                  pl.BlockSpec((1, TM, D), row), pl.BlockSpec((1, TM, A), row),
                  _const_spec(mod.shape),
                  lspec(n2w.shape),
                  lspec(pattn.shape), lspec(wout.shape), lspec(wg.shape), lspec(wu.shape), lspec(wd.shape)],
        out_specs=pl.BlockSpec((1, TM, D), row),
        out_shape=jax.ShapeDtypeStruct((B, S, D), F32),
        scratch_shapes=[pltpu.VMEM((TM, d_ff), BF16)],
        compiler_params=_params(2),
        name="merge_swiglu",
    )(x, part, ga, ya, mod, n2w, pattn, wout, wg, wu, wd)


def kernel(x, c, positions, norm1_w, norm2_w, w_ada, b_ada, w_in, pool_w, pool_scale, conv_w,
           q_norm_w, k_norm_w, p_pool, p_conv, p_attn, w_out, w_gate, w_up, w_down):
    B, S, D = x.shape
    L = w_ada.shape[0]
    assert S % TM == 0 and TM % SUB == 0 and SUB % MOBA_BLOCK == 0 and HALO >= max(POOL_WINDOWS) - 1
    n_groups, gdim = pool_w.shape[1], pool_w.shape[2]
    assert n_groups == len(POOL_WINDOWS) and conv_w.shape[1] == CONV_K

    assert w_ada.shape[2] == N_MOD * D
    b_ada3 = b_ada.reshape(L, 1, N_MOD * D)
    in_f32 = (w_in, p_pool, p_conv)
    post_f32 = (p_attn, w_out, w_gate, w_up, w_down)
    mod, in_bf = _modulation(0, c, w_ada, b_ada3, [(w, 0) for w in in_f32])
    cos_t, sin_t = _rope_tables(positions)

    pwbd = jnp.zeros((L, n_groups * gdim, n_groups * gdim), F32)
    for g in range(n_groups):
        pwbd = pwbd.at[:, g * gdim:(g + 1) * gdim, g * gdim:(g + 1) * gdim].set(pool_w[:, g])
    pwbd = pwbd.astype(BF16)
    qw = q_norm_w.reshape(L, HEAD_DIM, 1)
    kw = k_norm_w.reshape(L, HEAD_DIM, 1)
    n1w, n2w, pscale = norm1_w, norm2_w, pool_scale

    for l in range(L):
        part, ga, qT, kx, vx, km = _in_call(l, x, mod, n1w, in_bf[0], qw, kw, cos_t, sin_t, pwbd,
                                            pscale, conv_w, in_bf[1], in_bf[2])
        more = l + 1 < L
        casts = [(w, l) for w in post_f32] + ([(w, l + 1) for w in in_f32] if more else [])
        ya, cast, mod_next = _attn_call(qT, kx, vx, km, casts, (l + 1, c, w_ada, b_ada3) if more else None)
        x = _post_call(l, x, part, ga, ya, mod, n2w, *cast[:len(post_f32)])
        in_bf, mod = cast[len(post_f32):], mod_next
    return x
```

```python
import functools
import math

import numpy as np
import jax
import jax.numpy as jnp
from jax import lax
from jax.experimental import pallas as pl
from jax.experimental.pallas import tpu as pltpu

F32 = jnp.float32
BF16 = jnp.bfloat16

HEAD_DIM = 64
ATTN_HEADS = 8
ATTN_WIDTH = ATTN_HEADS * HEAD_DIM
ROPE_DIM = HEAD_DIM // 4
ROPE_HALF = ROPE_DIM // 2
ROPE_THETA = 500000.0
MOBA_BLOCK = 256
MOBA_TOPK = 3
POOL_WINDOWS = (2, 4, 8, 16)
CONV_K = 3
EPS = 1e-6
NEG = -1e30

LANES = 128
PAIR = LANES
HALO = 16
VMEM_LIMIT = 56 * 1024 * 1024

TM = 512
SUB = TM
FF_TN = 256
FF_AHEAD = 2
GATE_TN = 512
PV_BEHIND = 1
QK_AHEAD = 4
KV_GROUP = 2
BF16_SUBLANES = 16
V_ROWS = HEAD_DIM + BF16_SUBLANES
Q_SCALE = (1.0 / math.sqrt(HEAD_DIM)) * math.log2(math.e)


def _dot(a, b):
    return jnp.dot(a, b, preferred_element_type=F32)


def _const_spec(shape):
    nd = len(shape)
    return pl.BlockSpec(shape, lambda *_: (0,) * nd, pipeline_mode=pl.Buffered(1))


def _sigmoid(x):
    return 0.5 * jnp.tanh(0.5 * x) + 0.5


def _layer_spec(layer, shape):
    if len(shape) == 2:
        return _const_spec(shape)
    nd = len(shape)
    return pl.BlockSpec((None,) + tuple(shape[1:]), lambda *_: (layer,) + (0,) * (nd - 1),
                        pipeline_mode=pl.Buffered(1))


def _params(n_axes):
    return pltpu.CompilerParams(dimension_semantics=("arbitrary",) * n_axes,
                                vmem_limit_bytes=VMEM_LIMIT)


N_MOD = 6


def _mod_chunk(c_ref, w_ref, b_ref, o_ref):
    c = c_ref[...]
    c_act = (c * jax.nn.sigmoid(c)).astype(BF16)
    o_ref[...] = _dot(c_act, w_ref[...].astype(BF16)) + b_ref[...]


def _mod_specs(layer, n_batch, d_model, step_to_chunk):
    ins = [pl.BlockSpec((n_batch, d_model), lambda i: (0, 0)),
           pl.BlockSpec((None, d_model, d_model), lambda i: (layer, 0, step_to_chunk(i))),
           pl.BlockSpec((None, 1, d_model), lambda i: (layer, 0, step_to_chunk(i)))]
    out = pl.BlockSpec((None, n_batch, d_model), lambda i: (step_to_chunk(i), 0, 0))
    shape = jax.ShapeDtypeStruct((N_MOD, n_batch, d_model), F32)
    return ins, out, shape


def _cast_specs(casts, n_steps):
    ins, outs, shapes = [], [], []
    for w, layer in casts:
        _, rows, cols = w.shape
        assert rows % (n_steps * BF16_SUBLANES) == 0
        ins.append(pl.BlockSpec((None, rows // n_steps, cols), lambda i, layer=layer: (layer, i, 0)))
        outs.append(pl.BlockSpec((rows // n_steps, cols), lambda i: (i, 0)))
        shapes.append(jax.ShapeDtypeStruct((rows, cols), BF16))
    return ins, outs, shapes


def _prologue_kernel(*refs, n_cast):
    _mod_chunk(*refs[:3], refs[3 + n_cast])
    for w_ref, wb_ref in zip(refs[3:3 + n_cast], refs[4 + n_cast:]):
        wb_ref[...] = w_ref[...].astype(BF16)


def _modulation(layer, c, w_ada, b_ada3, casts):
    B, D = c.shape
    n_steps = max(B, N_MOD)
    ins, out, shape = _mod_specs(layer, B, D, lambda i: jnp.minimum(i, N_MOD - 1))
    cast_in, cast_out, cast_shape = _cast_specs(casts, n_steps)
    res = pl.pallas_call(
        functools.partial(_prologue_kernel, n_cast=len(casts)), grid=(n_steps,),
        in_specs=ins + cast_in, out_specs=[out] + cast_out, out_shape=[shape] + cast_shape,
        compiler_params=_params(1), name="adaln_modulation",
    )(c, w_ada, b_ada3, *[w for w, _ in casts])
    return res[0], res[1:]


def _rope_kernel(pos_ref, freq_ref, c_ref, s_ref):
    ang = freq_ref[...] * pos_ref[0].astype(F32)
    c_ref[0] = jnp.cos(ang)
    s_ref[0] = jnp.sin(ang)


def _rope_tables(positions):
    B, S = positions.shape
    freq = ROPE_THETA ** (-np.arange(0, ROPE_DIM, 2, dtype=np.float64) / ROPE_DIM)
    freq = jnp.asarray(freq.reshape(ROPE_HALF, 1), F32)
    spec = pl.BlockSpec((1, ROPE_HALF, S), lambda b: (b, 0, 0))
    shp = jax.ShapeDtypeStruct((B, ROPE_HALF, S), F32)
    return pl.pallas_call(
        _rope_kernel,
        grid=(B,),
        in_specs=[pl.BlockSpec((1, 1, S), lambda b: (b, 0, 0)),
                  pl.BlockSpec((ROPE_HALF, 1), lambda b: (0, 0))],
        out_specs=(spec, spec),
        out_shape=(shp, shp),
        compiler_params=_params(1),
        name="rotary_tables",
    )(positions.reshape(B, 1, S), freq)


def _in_kernel(x_ref, mod_ref, n1w_ref, win_ref, qw_ref, kw_ref,
               cos_ref, sin_ref, pwbd_ref, pscale_ref, convw_ref, ppool_ref, pconv_ref,
               part_ref, ga_ref, qT_ref, kx_ref, vx_ref, km_ref, ext_ref,
               *, layer, d_model, pool_w, conv_w, attn_w):
    si = pl.program_id(1)
    n_sub = TM // SUB
    hs = {}

    def adaln(sub):
        x = x_ref[0, sub * SUB:(sub + 1) * SUB, :]
        ms = jnp.mean(x * x, axis=-1, keepdims=True)
        sh1 = mod_ref[0, pl.ds(pl.program_id(0), 1), :]
        sc1 = mod_ref[1, pl.ds(pl.program_id(0), 1), :]
        gain = n1w_ref[layer:layer + 1, :]
        hs[sub] = ((x * lax.rsqrt(ms + EPS)) * (gain * (1.0 + sc1)) + sh1).astype(BF16)

    @pl.when(si == 0)
    def _():
        ext_ref[0:HALO, :] = jnp.zeros((HALO, pool_w + conv_w), F32)
        km_ref[0] = jnp.zeros(km_ref.shape[1:], F32)

    o_pool = 0
    o_uc = o_pool + pool_w
    o_bc = o_uc + conv_w
    o_cc = o_bc + conv_w
    o_q = o_cc + conv_w
    o_k = o_q + attn_w
    o_v = o_k + attn_w
    o_g = o_v + attn_w
    n_pairs = attn_w // PAIR

    def sub_stages(sub):
        rows = slice(sub * SUB, (sub + 1) * SUB)
        blk0 = sub * (SUB // MOBA_BLOCK)
        cos_t = cos_ref[0, :, rows]
        sin_t = sin_ref[0, :, rows]

        def norm_rope_t(u, gain_ref):
            gain = gain_ref[...]
            outs = []
            for p in range(n_pairs):
                u_t = u[:, p * PAIR:(p + 1) * PAIR].T
                parts = []
                for hh in range(PAIR // HEAD_DIM):
                    xh = u_t[hh * HEAD_DIM:(hh + 1) * HEAD_DIM, :]
                    xn = xh * lax.rsqrt(jnp.mean(xh * xh, axis=0, keepdims=True) + EPS) * gain
                    x1 = xn[0:ROPE_HALF, :]
                    x2 = xn[ROPE_HALF:ROPE_DIM, :]
                    parts += [x1 * cos_t - x2 * sin_t, x2 * cos_t + x1 * sin_t, xn[ROPE_DIM:, :]]
                outs.append(jnp.concatenate(parts, axis=0))
            return outs

        def proj(lo, hi):
            return lambda: (_dot(hs[sub], win_ref[:, lo:hi]),)

        def q_stage(uq):
            for p, qp_t in enumerate(norm_rope_t(uq, qw_ref)):
                qT_ref[0, p * PAIR:(p + 1) * PAIR, rows] = (qp_t * Q_SCALE).astype(BF16)

        def k_stage(uk):
            row_blk = (si * (TM // MOBA_BLOCK) + blk0
                       + lax.broadcasted_iota(jnp.int32, (SUB, PAIR), 0) // MOBA_BLOCK)
            onehot = jnp.where(lax.broadcasted_iota(jnp.int32, (SUB, PAIR), 1) == row_blk, 1.0, 0.0).astype(BF16)
            for p, kp_t in enumerate(norm_rope_t(uk, kw_ref)):
                kp = kp_t.T
                kx_ref[0, rows, 2 * p * PAIR:(2 * p + 1) * PAIR] = kp.astype(BF16)
                kx_ref[0, rows, (2 * p + 1) * PAIR:(2 * p + 2) * PAIR] = onehot
                for j in range(SUB // MOBA_BLOCK):
                    lanes = slice(p * PAIR, (p + 1) * PAIR)
                    mean = jnp.mean(kp[j * MOBA_BLOCK:(j + 1) * MOBA_BLOCK, :], axis=0, keepdims=True)
                    blk_row = lax.broadcasted_iota(jnp.int32, (km_ref.shape[1], PAIR), 0)
                    mine = blk_row == si * (TM // MOBA_BLOCK) + blk0 + j
                    km_ref[0, :, lanes] = jnp.where(mine, mean, km_ref[0, :, lanes])

        def v_stage(uv):
            ones_rows = jnp.where(lax.broadcasted_iota(jnp.int32, (V_ROWS - HEAD_DIM, SUB), 0) == 0,
                                  1.0, 0.0).astype(BF16)
            for p in range(n_pairs):
                vT = uv[:, p * PAIR:(p + 1) * PAIR].T.astype(BF16)
                for hh in range(PAIR // HEAD_DIM):
                    r0 = (p * (PAIR // HEAD_DIM) + hh) * V_ROWS
                    vx_ref[0, r0:r0 + HEAD_DIM, rows] = vT[hh * HEAD_DIM:(hh + 1) * HEAD_DIM, :]
                    vx_ref[0, r0 + HEAD_DIM:r0 + V_ROWS, rows] = ones_rows

        mixed = {}

        def mixer_stage(ua):
            up = ua[:, o_pool:o_uc]
            uc = ua[:, o_uc:o_bc]
            bc = ua[:, o_bc:o_cc]
            cc = ua[:, o_cc:o_q]
            ext_ref[HALO:HALO + SUB, 0:pool_w] = up
            ext_ref[HALO:HALO + SUB, pool_w:pool_w + conv_w] = cc * uc
            ext = ext_ref[...]
            ext_ref[0:HALO, :] = ext[SUB:SUB + HALO, :]
            e_p = ext[:, 0:pool_w]
            e_c = ext[:, pool_w:pool_w + conv_w]

            s2 = e_p + pltpu.roll(e_p, 1, 0)
            s4 = s2 + pltpu.roll(s2, 2, 0)
            s8 = s4 + pltpu.roll(s4, 4, 0)
            s16 = s8 + pltpu.roll(s8, 8, 0)
            gdim = pool_w // len(POOL_WINDOWS)
            lane = lax.broadcasted_iota(jnp.int32, (1, pool_w), 1)
            wsum = jnp.where(lane < gdim, s2, jnp.where(lane < 2 * gdim, s4, jnp.where(lane < 3 * gdim, s8, s16)))
            win = jnp.where(lane < gdim, POOL_WINDOWS[0],
                            jnp.where(lane < 2 * gdim, POOL_WINDOWS[1],
                                      jnp.where(lane < 3 * gdim, POOL_WINDOWS[2], POOL_WINDOWS[3])))
            t = si * TM + sub * SUB + lax.broadcasted_iota(jnp.int32, (SUB, pool_w), 0)
            cnt = jnp.minimum(t + 1, win).astype(F32)
            pooled = wsum[HALO:, :] / cnt - up
            yp = _dot(pooled.astype(BF16), pwbd_ref[...]) * pscale_ref[layer:layer + 1, :]
            mixed["pool"] = _dot(yp.astype(BF16), ppool_ref[...])

            cw = convw_ref[...]
            conv = cw[0:1, :] * pltpu.roll(e_c, 2, 0) + cw[1:2, :] * pltpu.roll(e_c, 1, 0) + cw[2:3, :] * e_c
            yc = bc * conv[HALO:, :]
            mixed["conv"] = _dot(yc.astype(BF16), pconv_ref[...])

        def gate_dots(c):
            cols = [slice(o_g + i * d_model + c * GATE_TN, o_g + i * d_model + (c + 1) * GATE_TN) for i in range(3)]
            return lambda: tuple(_dot(hs[sub], win_ref[:, cs]) for cs in cols)

        def gate_stage(c):
            out = slice(c * GATE_TN, (c + 1) * GATE_TN)

            def run(g0, g1, g2):
                merged = _sigmoid(g0) * mixed["pool"][:, out] + _sigmoid(g1) * mixed["conv"][:, out]
                part_ref[0, rows, out] = merged.astype(BF16)
                ga_ref[0, rows, out] = _sigmoid(g2).astype(BF16)
            return run

        stages = [(proj(o_q, o_k), q_stage), (proj(o_k, o_v), k_stage)]
        if sub + 1 < n_sub:
            stages.append((lambda: (), lambda: adaln(sub + 1)))
        stages += [(proj(o_v, o_g), v_stage), (proj(o_pool, o_q), mixer_stage)]
        stages += [(gate_dots(c), gate_stage(c)) for c in range(d_model // GATE_TN)]
        return stages

    adaln(0)
    stages = [st for sub in range(n_sub) for st in sub_stages(sub)]
    res = stages[0][0]()
    for i, (_, finish) in enumerate(stages):
        nxt = stages[i + 1][0]() if i + 1 < len(stages) else None
        finish(*res)
        res = nxt


def _in_call(layer, x, mod, n1w, w_in, qw, kw, cos_t, sin_t, pwbd, pscale, convw, ppool, pconv):
    B, S, D = x.shape
    pool_w = pwbd.shape[-1]
    conv_w = convw.shape[-1]
    A = ATTN_WIDTH
    nt = S // TM
    row = lambda b, s: (b, s, 0)
    tr = lambda b, s: (b, 0, s)
    lspec = functools.partial(_layer_spec, layer)
    kern = functools.partial(_in_kernel, layer=layer, d_model=D, pool_w=pool_w, conv_w=conv_w, attn_w=A)
    return pl.pallas_call(
        kern,
        grid=(B, nt),
        in_specs=[pl.BlockSpec((1, TM, D), row),
                  _const_spec(mod.shape),
                  lspec(n1w.shape), lspec(w_in.shape),
                  lspec(qw.shape), lspec(kw.shape),
                  pl.BlockSpec((1, ROPE_HALF, TM), tr), pl.BlockSpec((1, ROPE_HALF, TM), tr),
                  lspec(pwbd.shape), lspec(pscale.shape), lspec(convw.shape),
                  lspec(ppool.shape), lspec(pconv.shape)],
        out_specs=(pl.BlockSpec((1, TM, D), row), pl.BlockSpec((1, TM, D), row),
                   pl.BlockSpec((1, A, TM), tr), pl.BlockSpec((1, TM, 2 * A), row),
                   pl.BlockSpec((1, ATTN_HEADS * V_ROWS, TM), tr),
                   pl.BlockSpec((1, S // MOBA_BLOCK, A), lambda b, s: (b, 0, 0))),
        out_shape=(jax.ShapeDtypeStruct((B, S, D), BF16), jax.ShapeDtypeStruct((B, S, D), BF16),
                   jax.ShapeDtypeStruct((B, A, S), BF16), jax.ShapeDtypeStruct((B, S, 2 * A), BF16),
                   jax.ShapeDtypeStruct((B, ATTN_HEADS * V_ROWS, S), BF16),
                   jax.ShapeDtypeStruct((B, S // MOBA_BLOCK, A), F32)),
        scratch_shapes=[pltpu.VMEM((SUB + HALO, pool_w + conv_w), F32)],
        compiler_params=_params(2),
        name="in_proj_mixers",
    )(x, mod, n1w, w_in, qw, kw, cos_t, sin_t, pwbd, pscale, convw, ppool, pconv)


def _attn_kernel(qT_ref, kx_ref, vx_ref, km_ref, *refs, n_blocks, n_cast, with_mod):
    n_in = n_cast + (3 if with_mod else 0)
    o_ref = refs[n_in]
    for w_ref, wb_ref in zip(refs[:n_cast], refs[n_in + 1:n_in + 1 + n_cast]):
        wb_ref[...] = w_ref[...].astype(BF16)
    if with_mod:
        _mod_chunk(*refs[n_cast:n_in], refs[n_in + 1 + n_cast])

    TQ = MOBA_BLOCK
    heads_per_pair = PAIR // HEAD_DIM
    km = km_ref[0]
    blk = lax.broadcasted_iota(jnp.int32, (n_blocks, TQ), 0)
    half = lax.broadcasted_iota(jnp.int32, (PAIR, TQ), 0) // HEAD_DIM
    bias_pad = jnp.zeros((BF16_SUBLANES - n_blocks, TQ), F32)
    rhs_pad = jnp.zeros((PAIR - BF16_SUBLANES, TQ), BF16)

    def make_rhs(qi, h):
        p, hh = divmod(h, heads_per_pair)
        lanes = slice(p * PAIR, (p + 1) * PAIR)
        qT_pair = qT_ref[0, lanes, qi * TQ:(qi + 1) * TQ]
        qT_h = jnp.where(half == hh, qT_pair, jnp.zeros_like(qT_pair))
        if qi <= MOBA_TOPK:
            sel = blk <= qi
        else:
            past = blk < qi
            km_pair = km[:, lanes]
            km_hi = km_pair.astype(BF16)
            km_lo = (km_pair - km_hi.astype(F32)).astype(BF16)
            gate = _dot(km_hi, qT_h) + _dot(km_lo, qT_h)
            gate = jnp.where(past, gate, NEG)
            beaten = jnp.zeros((n_blocks, TQ), jnp.int32)
            for m in range(n_blocks):
                gm = gate[m:m + 1, :]
                ahead = (gm > gate) | ((gm == gate) & (m < blk))
                beaten = beaten + ahead.astype(jnp.int32)
            sel = (past & (beaten < MOBA_TOPK)) | (blk == qi)
        bias = jnp.concatenate([jnp.where(sel, 0.0, NEG), bias_pad], axis=0).astype(BF16)
        return jnp.concatenate([qT_h, bias, rhs_pad], axis=0)

    items = []
    for qi in range(n_blocks):
        hi = qi
        while hi >= 0:
            lo = max(hi - KV_GROUP + 1, 0)
            items += [(qi, lo, hi, h) for h in range(ATTN_HEADS)]
            hi = lo - 1
    rhs = {}
    run_max = {}
    run_acc = {}
    done = {}

    def causal_mask(n_keys):
        kpos = lax.broadcasted_iota(jnp.int32, (n_keys, TQ), 0)
        qpos = lax.broadcasted_iota(jnp.int32, (n_keys, TQ), 1)
        return kpos <= qpos + (n_keys - TQ)

    def scores(item):
        qi, lo, hi, h = item
        if (qi, h) not in rhs:
            rhs[(qi, h)] = make_rhs(qi, h)
        p = h // heads_per_pair
        s = _dot(kx_ref[0, lo * TQ:(hi + 1) * TQ, 2 * p * PAIR:(2 * p + 2) * PAIR], rhs[(qi, h)])
        return s.astype(BF16)

    def values(qi, lo, hi, h, pr, alpha):
        acc = _dot(vx_ref[0, h * V_ROWS:(h + 1) * V_ROWS, lo * TQ:(hi + 1) * TQ], pr)
        if alpha is not None:
            acc = alpha * run_acc[(qi, h)] + acc
        run_acc[(qi, h)] = acc
        if lo == 0:
            done[h] = acc[0:HEAD_DIM, :] / acc[HEAD_DIM:HEAD_DIM + 1, :]
            del run_acc[(qi, h)], run_max[(qi, h)], rhs[(qi, h)]
            if h == ATTN_HEADS - 1:
                oT = jnp.concatenate([done[j] for j in range(ATTN_HEADS)], axis=0)
                o_ref[0, qi * TQ:(qi + 1) * TQ, :] = oT.T.astype(BF16)

    pending = [scores(it) for it in items[:QK_AHEAD]]
    late = []
    for i, (qi, lo, hi, h) in enumerate(items):
        if i + QK_AHEAD < len(items):
            pending.append(scores(items[i + QK_AHEAD]))
        s = pending.pop(0)
        if hi == qi:
            s = jnp.where(causal_mask((hi - lo + 1) * TQ), s, jnp.asarray(NEG, BF16))
        m_n = jnp.max(s, axis=0, keepdims=True).astype(F32)
        alpha = None
        if (qi, h) in run_max:
            m_n = jnp.maximum(run_max[(qi, h)], m_n)
            alpha = jnp.exp2(run_max[(qi, h)] - m_n)
        run_max[(qi, h)] = m_n
        late.append((qi, lo, hi, h, jnp.exp2(s - m_n.astype(BF16)), alpha))
        if len(late) > PV_BEHIND:
            values(*late.pop(0))
    for job in late:
        values(*job)


def _attn_call(qT, kx, vx, km, casts, mod_job):
    B, A, S = qT.shape
    nb = S // MOBA_BLOCK
    assert nb <= BF16_SUBLANES and nb <= PAIR
    kern = functools.partial(_attn_kernel, n_blocks=nb, n_cast=len(casts), with_mod=mod_job is not None)
    whole = lambda b: (b, 0, 0)
    mod_in, mod_out, mod_shape, mod_args = [], [], [], []
    if mod_job is not None:
        layer, c, w_ada, b_ada3 = mod_job
        assert B >= N_MOD
        mod_in, out, shape = _mod_specs(layer, B, c.shape[1], lambda i: jnp.minimum(i, N_MOD - 1))
        mod_out, mod_shape, mod_args = [out], [shape], [c, w_ada, b_ada3]
    cast_in, cast_out, cast_shape = _cast_specs(casts, B)
    out = pl.pallas_call(
        kern,
        grid=(B,),
        in_specs=[pl.BlockSpec((1, A, S), whole),
                  pl.BlockSpec((1, S, 2 * A), whole),
                  pl.BlockSpec((1, ATTN_HEADS * V_ROWS, S), whole),
                  pl.BlockSpec((1, nb, A), whole)] + cast_in + mod_in,
        out_specs=[pl.BlockSpec((1, S, A), whole)] + cast_out + mod_out,
        out_shape=[jax.ShapeDtypeStruct((B, S, A), BF16)] + cast_shape + mod_shape,
        compiler_params=_params(1),
        name="moba_attention",
    )(qT, kx, vx, km, *[w for w, _ in casts], *mod_args)
    n_cast = len(casts)
    return out[0], out[1:1 + n_cast], (out[1 + n_cast] if mod_job is not None else None)


def _post_kernel(x_ref, part_ref, ga_ref, ya_ref, mod_ref, n2w_ref,
                 pattn_ref, wout_ref, wg_ref, wu_ref, wd_ref, o_ref, act_ref, *, layer):
    row = pl.ds(pl.program_id(0), 1)
    g1 = mod_ref[2, row, :]
    sh2 = mod_ref[3, row, :]
    sc2 = mod_ref[4, row, :]
    g2 = mod_ref[5, row, :]
    merged = part_ref[0].astype(F32) + ga_ref[0].astype(F32) * _dot(ya_ref[0], pattn_ref[...])
    x1 = x_ref[0] + g1 * _dot(merged.astype(BF16), wout_ref[...])
    ms = jnp.mean(x1 * x1, axis=-1, keepdims=True)
    h2 = (x1 * lax.rsqrt(ms + EPS)) * (n2w_ref[layer:layer + 1, :] * (1.0 + sc2)) + sh2
    h2 = h2.astype(BF16)
    d_ff = wg_ref.shape[1]

    def ff_dots(j):
        cols = slice(j * FF_TN, (j + 1) * FF_TN)
        return _dot(h2, wg_ref[:, cols]), _dot(h2, wu_ref[:, cols])

    n_chunks = d_ff // FF_TN
    pending = [ff_dots(j) for j in range(min(FF_AHEAD, n_chunks))]
    for j in range(n_chunks):
        if j + FF_AHEAD < n_chunks:
            pending.append(ff_dots(j + FF_AHEAD))
        g, u = pending.pop(0)
        act_ref[:, j * FF_TN:(j + 1) * FF_TN] = (g * _sigmoid(g) * u).astype(BF16)
    o_ref[0] = x1 + g2 * _dot(act_ref[...], wd_ref[...])


def _post_call(layer, x, part, ga, ya, mod, n2w, pattn, wout, wg, wu, wd):
    B, S, D = x.shape
    A = ya.shape[-1]
    d_ff = wg.shape[-1]
    assert d_ff % FF_TN == 0
    row = lambda b, s: (b, s, 0)
    lspec = functools.partial(_layer_spec, layer)
    return pl.pallas_call(
        functools.partial(_post_kernel, layer=layer),
        grid=(B, S // TM),
        in_specs=[pl.BlockSpec((1, TM, D), row), pl.BlockSpec((1, TM, D), row),
                  pl.BlockSpec((1, TM, D), row), pl.BlockSpec((1, TM, A), row),
                  _const_spec(mod.shape),
                  lspec(n2w.shape),
                  lspec(pattn.shape), lspec(wout.shape), lspec(wg.shape), lspec(wu.shape), lspec(wd.shape)],
        out_specs=pl.BlockSpec((1, TM, D), row),
        out_shape=jax.ShapeDtypeStruct((B, S, D), F32),
        scratch_shapes=[pltpu.VMEM((TM, d_ff), BF16)],
        compiler_params=_params(2),
        name="merge_swiglu",
    )(x, part, ga, ya, mod, n2w, pattn, wout, wg, wu, wd)


def kernel(x, c, positions, norm1_w, norm2_w, w_ada, b_ada, w_in, pool_w, pool_scale, conv_w,
           q_norm_w, k_norm_w, p_pool, p_conv, p_attn, w_out, w_gate, w_up, w_down):
    B, S, D = x.shape
    L = w_ada.shape[0]
    assert S % TM == 0 and TM % SUB == 0 and SUB % MOBA_BLOCK == 0 and HALO >= max(POOL_WINDOWS) - 1
    n_groups, gdim = pool_w.shape[1], pool_w.shape[2]
    assert n_groups == len(POOL_WINDOWS) and conv_w.shape[1] == CONV_K

    assert w_ada.shape[2] == N_MOD * D
    b_ada3 = b_ada.reshape(L, 1, N_MOD * D)
    in_f32 = (w_in, p_pool, p_conv)
    post_f32 = (p_attn, w_out, w_gate, w_up, w_down)
    mod, in_bf = _modulation(0, c, w_ada, b_ada3, [(w, 0) for w in in_f32])
    cos_t, sin_t = _rope_tables(positions)

    pwbd = jnp.zeros((L, n_groups * gdim, n_groups * gdim), F32)
    for g in range(n_groups):
        pwbd = pwbd.at[:, g * gdim:(g + 1) * gdim, g * gdim:(g + 1) * gdim].set(pool_w[:, g])
    pwbd = pwbd.astype(BF16)
    qw = q_norm_w.reshape(L, HEAD_DIM, 1)
    kw = k_norm_w.reshape(L, HEAD_DIM, 1)
    n1w, n2w, pscale = norm1_w, norm2_w, pool_scale

    for l in range(L):
        part, ga, qT, kx, vx, km = _in_call(l, x, mod, n1w, in_bf[0], qw, kw, cos_t, sin_t, pwbd,
                                            pscale, conv_w, in_bf[1], in_bf[2])
        more = l + 1 < L
        casts = [(w, l) for w in post_f32] + ([(w, l + 1) for w in in_f32] if more else [])
        ya, cast, mod_next = _attn_call(qT, kx, vx, km, casts, (l + 1, c, w_ada, b_ada3) if more else None)
        x = _post_call(l, x, part, ga, ya, mod, n2w, *cast[:len(post_f32)])
        in_bf, mod = cast[len(post_f32):], mod_next
    return x
```
